```python
import jax, jax.numpy as jnp
from jax import lax
import numpy as np

D_MODEL = 1024
BATCH = 8
SEQ = 4096
DEPTH = 2

MEM_LEN = 256
GRID_W = 64
EPS = 1e-6

GLA_HEADS = 4
GLA_DK = 128
GLA_DV = 256
GLA_QK = GLA_HEADS * GLA_DK
GLA_V = GLA_HEADS * GLA_DV
GLA_RANK = 16
GLA_TAU = 16.0
GLA_CHUNK = 64

NA_HEADS = 8
NA_HD = 64
NA_W = NA_HEADS * NA_HD
NA_ROWS = 8
NA_COLS = 16

MEM_HEADS = 4
MEM_HD = 128
MEM_W = MEM_HEADS * MEM_HD

N_BRANCH = 3

IN_SPLIT_SIZES = (GLA_QK, GLA_QK, GLA_V, GLA_V, GLA_RANK, GLA_RANK,
                  NA_W, NA_W, NA_W, NA_W,
                  MEM_W, MEM_W,
                  N_BRANCH * D_MODEL)
IN_COLS = sum(IN_SPLIT_SIZES)

kernel_name = "hybrid_gla_natten_memory_gated_encoder"


def rmsnorm(x, g):
    xf = x.astype(jnp.float32)
    y = xf * lax.rsqrt(jnp.mean(xf * xf, axis=-1, keepdims=True) + EPS)
    return (y * g.astype(jnp.float32)).astype(x.dtype)


def gla_scan(q, k, v, g):
    B, H, T, dk = q.shape
    dv = v.shape[-1]
    C = GLA_CHUNK
    NC = T // C

    def chunks(a):
        return jnp.moveaxis(a.reshape(B, H, NC, C, a.shape[-1]), 2, 0)

    qc, kc, vc = chunks(q), chunks(k), chunks(v)
    bc = jnp.cumsum(chunks(g), axis=3)
    mask = jnp.tril(jnp.ones((C, C), dtype=bool))[:, :, None]

    def step(S, inp):
        qi, ki, vi, bi = inp
        o_inter = jnp.einsum('bhik,bhkv->bhiv', qi * jnp.exp(bi), S)
        diff = bi[:, :, :, None, :] - bi[:, :, None, :, :]
        decay = jnp.exp(jnp.where(mask, diff, -jnp.inf))
        A = jnp.einsum('bhijk,bhjk->bhij', qi[:, :, :, None, :] * decay, ki)
        o = o_inter + jnp.einsum('bhij,bhjv->bhiv', A, vi)
        b_last = bi[:, :, -1:, :]
        S = jnp.exp(b_last[:, :, 0, :])[..., None] * S + jnp.einsum(
            'bhjk,bhjv->bhkv', ki * jnp.exp(b_last - bi), vi)
        return S, o

    S0 = jnp.zeros((B, H, dk, dv), jnp.float32)
    _, o = lax.scan(step, S0, (qc, kc, vc, bc))
    return jnp.moveaxis(o, 0, 2).reshape(B, H, T, dv)


def neighbourhood_attention(q, k, v, rpb):
    B, S, H, hd = q.shape
    rows = S // GRID_W
    kr = min(NA_ROWS, rows)
    qg = q.reshape(B, rows, GRID_W, H, hd)
    kg = k.reshape(B, rows, GRID_W, H, hd)
    vg = v.reshape(B, rows, GRID_W, H, hd)
    col = np.arange(GRID_W)
    cs = np.clip(col - NA_COLS // 2, 0, GRID_W - NA_COLS)
    col_idx = cs[:, None] + np.arange(NA_COLS)[None, :]
    dc_idx = (col_idx - col[:, None]) + (NA_COLS - 1)

    def row_block(r):
        rs = jnp.clip(r - kr // 2, 0, rows - kr)
        q_row = lax.dynamic_index_in_dim(qg, r, axis=1, keepdims=False)
        k_band = lax.dynamic_slice_in_dim(kg, rs, kr, axis=1)
        v_band = lax.dynamic_slice_in_dim(vg, rs, kr, axis=1)
        k_nb = k_band[:, :, col_idx]
        v_nb = v_band[:, :, col_idx]
        dr_idx = rs + jnp.arange(kr) - r + (NA_ROWS - 1)
        bias = rpb[:, dr_idx[None, :, None], dc_idx[:, None, :]]
        s = jnp.einsum('bqhd,bkqjhd->bhqkj', q_row, k_nb).astype(jnp.float32) \
            + bias.astype(jnp.float32)
        p = jax.nn.softmax(s.reshape(B, H, GRID_W, kr * NA_COLS), axis=-1)
        p = p.reshape(s.shape).astype(v.dtype)
        return jnp.einsum('bhqkj,bkqjhd->bqhd', p, v_nb)

    out = lax.map(row_block, jnp.arange(rows))
    return jnp.moveaxis(out, 0, 1).reshape(B, S, H * hd)


def memory_attention(q, k, v):
    s = jnp.einsum('bshd,bmhd->bhsm', q, k).astype(jnp.float32)
    p = jax.nn.softmax(s, axis=-1).astype(v.dtype)
    return jnp.einsum('bhsm,bmhd->bshd', p, v)


def setup_inputs(seed: int = 0) -> dict:
    key = jax.random.key(seed)
    ks = jax.random.split(key, 24)
    L, D = DEPTH, D_MODEL
    n = lambda k, s, sc: jax.random.normal(k, s, jnp.float32) * sc
    gain = lambda k, s: 1.0 + 0.05 * jax.random.normal(k, s, jnp.float32)
    return {
        "x": jax.random.normal(ks[0], (BATCH, SEQ, D), jnp.float32),
        "mem": jax.random.normal(ks[1], (BATCH, MEM_LEN, D), jnp.float32),
        "norm_g": gain(ks[2], (L, D)),
        "w_in": n(ks[3], (L, D, IN_COLS), D ** -0.5),
        "gla_w2_f": n(ks[4], (L, GLA_RANK, GLA_QK), GLA_RANK ** -0.5),
        "gla_b_f": n(ks[5], (L, GLA_QK), 0.1),
        "gla_w2_b": n(ks[6], (L, GLA_RANK, GLA_QK), GLA_RANK ** -0.5),
        "gla_b_b": n(ks[7], (L, GLA_QK), 0.1),
        "gla_out_g": gain(ks[8], (L, GLA_DV)),
        "p_a": n(ks[9], (L, GLA_V, D), GLA_V ** -0.5),
        "na_q_g": gain(ks[10], (L, NA_HD)),
        "na_k_g": gain(ks[11], (L, NA_HD)),
        "na_rpb": n(ks[12], (L, NA_HEADS, 2 * NA_ROWS - 1, 2 * NA_COLS - 1), 0.1),
        "p_b": n(ks[13], (L, NA_W, D), NA_W ** -0.5),
        "mem_norm_g": gain(ks[14], (L, D)),
        "w_mem_kv": n(ks[15], (L, D, 2 * MEM_W), D ** -0.5),
        "mem_q_g": gain(ks[16], (L, MEM_HD)),
        "mem_k_g": gain(ks[17], (L, MEM_HD)),
        "p_c": n(ks[18], (L, MEM_W, D), MEM_W ** -0.5),
        "w_out": n(ks[19], (L, D, D), D ** -0.5),
    }


def reference(x, mem, norm_g, w_in, gla_w2_f, gla_b_f, gla_w2_b, gla_b_b, gla_out_g, p_a,
              na_q_g, na_k_g, na_rpb, p_b, mem_norm_g, w_mem_kv, mem_q_g, mem_k_g, p_c,
              w_out):
    B, S, D = x.shape
    M = mem.shape[1]
    f32 = jnp.float32
    split_idx = np.cumsum(IN_SPLIT_SIZES)[:-1]

    def heads_first(a, h, d):
        return a.reshape(B, S, h, d).transpose(0, 2, 1, 3).astype(f32)

    for l in range(DEPTH):
        h = rmsnorm(x, norm_g[l])
        proj = h @ w_in[l]
        (gq, gk, gv, ggate, glr_f, glr_b, nq, nk, nv, ngate, mq, mgate, merge) = \
            jnp.split(proj, split_idx, axis=-1)

        qa = heads_first(gq, GLA_HEADS, GLA_DK) * (GLA_DK ** -0.5)
        ka = heads_first(gk, GLA_HEADS, GLA_DK)
        va = heads_first(gv, GLA_HEADS, GLA_DV)
        g_f = jax.nn.log_sigmoid((glr_f @ gla_w2_f[l] + gla_b_f[l]).astype(f32)) / GLA_TAU
        g_b = jax.nn.log_sigmoid((glr_b @ gla_w2_b[l] + gla_b_b[l]).astype(f32)) / GLA_TAU
        g_f = heads_first(g_f, GLA_HEADS, GLA_DK)
        g_b = heads_first(g_b, GLA_HEADS, GLA_DK)
        o_fwd = gla_scan(qa, ka, va, g_f)
        o_bwd = jnp.flip(gla_scan(jnp.flip(qa, 2), jnp.flip(ka, 2), jnp.flip(va, 2),
                                  jnp.flip(g_b, 2)), 2)
        oa = (o_fwd + o_bwd).transpose(0, 2, 1, 3)
        oa = rmsnorm(oa, gla_out_g[l]).reshape(B, S, GLA_V).astype(x.dtype)
        ya = (oa * jax.nn.silu(ggate)) @ p_a[l]

        qb = rmsnorm(nq.reshape(B, S, NA_HEADS, NA_HD), na_q_g[l]) * (NA_HD ** -0.5)
        kb = rmsnorm(nk.reshape(B, S, NA_HEADS, NA_HD), na_k_g[l])
        vb = nv.reshape(B, S, NA_HEADS, NA_HD)
        ob = neighbourhood_attention(qb, kb, vb, na_rpb[l])
        yb = (ob * jax.nn.silu(ngate)) @ p_b[l]

        mem_kv = rmsnorm(mem, mem_norm_g[l]) @ w_mem_kv[l]
        mk, mv = jnp.split(mem_kv, 2, axis=-1)
        kc = rmsnorm(mk.reshape(B, M, MEM_HEADS, MEM_HD), mem_k_g[l])
        vc = mv.reshape(B, M, MEM_HEADS, MEM_HD)
        qc = rmsnorm(mq.reshape(B, S, MEM_HEADS, MEM_HD), mem_q_g[l]) * (MEM_HD ** -0.5)
        oc = memory_attention(qc, kc, vc).reshape(B, S, MEM_W)
        yc = (oc * jax.nn.silu(mgate)) @ p_c[l]

        gate_a, gate_b, gate_c = jnp.split(jax.nn.sigmoid(merge), N_BRANCH, axis=-1)
        y = gate_a * ya + gate_b * yb + gate_c * yc
        x = x + y @ w_out[l]
    return x
```

```python
import functools

import numpy as np
import jax
import jax.numpy as jnp
from jax import lax
from jax.experimental import pallas as pl
from jax.experimental.pallas import tpu as pltpu

F32 = jnp.float32
BF16 = jnp.bfloat16

EPS = 1e-6
NEG_BIG = -1e30

GRID_W = 64
GLA_HEADS, GLA_DK, GLA_DV = 4, 128, 256
GLA_RANK = 16
GLA_TAU = 16.0
GLA_CHUNK = 64
GLA_LEVELS = (8, 16, 32, 64)
NA_HEADS, NA_HD = 8, 64
NA_ROWS, NA_COLS = 8, 16
MEM_HEADS, MEM_HD = 4, 128

COL_TILE = 512
N_MAIN = 9216
T_GQ, T_GK, T_GV, T_GGATE = 0, 1, 2, 4
T_NQ, T_NK, T_NV, T_NGATE = 6, 7, 8, 9
T_MQ, T_MGATE, T_MERGE = 10, 11, 12
N_COL_TILES = N_MAIN // COL_TILE

VMEM_LIMIT_BYTES = 48 * 1024 * 1024


def _dot(a, b):
    return jnp.dot(a, b, preferred_element_type=F32)


def _dot_nt(a, b):
    return lax.dot_general(a, b, (((1,), (1,)), ((), ())), preferred_element_type=F32)


def _dot_tn(a, b):
    return lax.dot_general(a, b, (((0,), (0,)), ((), ())), preferred_element_type=F32)


def _in_proj_kernel(x_ref, g_ref, w_ref, wlr_ref, gain_ref, bd64_ref, bd128_ref,
                    proj_ref, glr_ref, h_ref):
    j = pl.program_id(1)

    @pl.when(j == 0)
    def _():
        xf = x_ref[...]
        ms = jnp.mean(xf * xf, axis=-1, keepdims=True)
        h = (xf * lax.rsqrt(ms + EPS) * g_ref[...]).astype(BF16)
        h_ref[...] = h
        glr_ref[...] = _dot(h, wlr_ref[...])

    acc = _dot(h_ref[...], w_ref[...])
    gain = gain_ref[0]

    is_silu = (j == T_GGATE) | (j == T_GGATE + 1) | (j == T_NGATE) | (j == T_MGATE)
    is_sig = j >= T_MERGE
    is_n64 = (j == T_NQ) | (j == T_NK)
    is_n128 = j == T_MQ
    is_lin = jnp.logical_not(is_silu | is_sig | is_n64 | is_n128)

    @pl.when(is_lin)
    def _():
        proj_ref[...] = (acc * gain).astype(BF16)

    @pl.when(is_silu)
    def _():
        proj_ref[...] = (acc * jax.nn.sigmoid(acc)).astype(BF16)

    @pl.when(is_sig)
    def _():
        proj_ref[...] = jax.nn.sigmoid(acc).astype(BF16)

    @pl.when(is_n64)
    def _():
        ms = _dot((acc * acc).astype(BF16), bd64_ref[...])
        proj_ref[...] = (acc * lax.rsqrt(ms + EPS) * gain).astype(BF16)

    @pl.when(is_n128)
    def _():
        ms = _dot((acc * acc).astype(BF16), bd128_ref[...])
        proj_ref[...] = (acc * lax.rsqrt(ms + EPS) * gain).astype(BF16)


def _in_proj(x2, g, w_main, w_lr, gains, bd64, bd128, tm):
    m, d = x2.shape
    return pl.pallas_call(
        _in_proj_kernel,
        out_shape=(jax.ShapeDtypeStruct((m, N_MAIN), BF16),
                   jax.ShapeDtypeStruct((m, 128), F32)),
        grid=(m // tm, N_COL_TILES),
        in_specs=[
            pl.BlockSpec((tm, d), lambda i, j: (i, 0)),
            pl.BlockSpec((1, d), lambda i, j: (0, 0)),
            pl.BlockSpec((d, COL_TILE), lambda i, j: (0, j)),
            pl.BlockSpec((d, 128), lambda i, j: (0, 0)),
            pl.BlockSpec((1, 1, COL_TILE), lambda i, j: (j, 0, 0)),
            pl.BlockSpec((COL_TILE, COL_TILE), lambda i, j: (0, 0)),
            pl.BlockSpec((COL_TILE, COL_TILE), lambda i, j: (0, 0)),
        ],
        out_specs=(pl.BlockSpec((tm, COL_TILE), lambda i, j: (i, j)),
                   pl.BlockSpec((tm, 128), lambda i, j: (i, 0))),
        scratch_shapes=[pltpu.VMEM((tm, d), BF16)],
        compiler_params=pltpu.CompilerParams(
            dimension_semantics=("arbitrary", "arbitrary"),
            vmem_limit_bytes=VMEM_LIMIT_BYTES),
        name="in_proj",
    )(x2, g, w_main, w_lr, gains, bd64, bd128)


def _gla_constants():
    c = GLA_CHUNK
    idx = np.arange(c)
    tri = np.zeros((2, 2 * len(GLA_LEVELS) * c, c), np.float32)
    msk = np.zeros((2, len(GLA_LEVELS), c, c), np.float32)
    for d in range(2):
        covered = np.zeros((c, c), bool)
        for n, l in enumerate(GLA_LEVELS):
            blk = idx // l
            same = blk[:, None] == blk[None, :]
            if d == 0:
                wq = same & (idx[None, :] <= idx[:, None])
                wk = same & (idx[None, :] > idx[:, None])
                adj = blk[:, None] == blk[None, :] + 1
            else:
                wq = same & (idx[None, :] >= idx[:, None])
                wk = same & (idx[None, :] < idx[:, None])
                adj = blk[None, :] == blk[:, None] + 1
            tri[d, n * c:(n + 1) * c] = wq
            tri[d, (len(GLA_LEVELS) + n) * c:(len(GLA_LEVELS) + n + 1) * c] = wk
            if l != c:
                m = adj & ~covered
                covered |= m
                msk[d, n] = m
    return jnp.asarray(tri, BF16), jnp.asarray(msk[:, :len(GLA_LEVELS) - 1], F32)


def _log_sigmoid(z):
    return jnp.minimum(z, 0.0) - jnp.log1p(jnp.exp(-jnp.abs(z)))


def _gla_chunk(d, head, r0, q_ref, k_ref, v_ref, g_scr, st_scr, o_ref, tri_ref, msk_ref):
    c, nl = GLA_CHUNK, len(GLA_LEVELS)
    ks = slice(head * GLA_DK, (head + 1) * GLA_DK)
    vs = slice(head * GLA_DV, (head + 1) * GLA_DV)
    rows = pl.ds(r0, c)
    q = q_ref[0, rows, ks].astype(F32)
    k = k_ref[0, rows, ks].astype(F32)
    v = v_ref[0, rows, vs]
    g = g_scr[rows, ks]

    g_hi = g.astype(BF16)
    g_lo = (g - g_hi.astype(F32)).astype(BF16)
    sums = _dot(tri_ref[d], jnp.concatenate([g_hi, g_lo], axis=1))
    e = jnp.exp(sums[:, :GLA_DK] + sums[:, GLA_DK:])

    def lvl(n):
        eq = e[n * c:(n + 1) * c]
        ek = e[(nl + n) * c:(nl + n + 1) * c]
        return (q * eq).astype(BF16), (k * ek).astype(BF16)

    a = None
    for n in range(nl - 1):
        qn, kn = lvl(n)
        t = msk_ref[d, n] * _dot_nt(qn, kn)
        a = t if a is None else a + t

    w8 = sums[0:c, :GLA_DK] + sums[0:c, GLA_DK:]
    row = lax.broadcasted_iota(jnp.int32, (8, GLA_DK), 0)
    lane = lax.broadcasted_iota(jnp.int32, (8, c), 1)
    diag_blocks = []
    for p in range(c // 8):
        wp, qp, kp = w8[8 * p:8 * p + 8], q[8 * p:8 * p + 8], k[8 * p:8 * p + 8]
        ad = jnp.zeros((8, c), F32)
        for jj in range(8):
            valid = (row >= jj) if d == 0 else (row <= jj)
            ex = jnp.exp(jnp.where(valid, wp - wp[jj:jj + 1], NEG_BIG))
            col = jnp.sum(qp * ex * kp[jj:jj + 1], axis=-1, keepdims=True)
            ad = jnp.where(lane == 8 * p + jj, col, ad)
        diag_blocks.append(ad)
    a = a + jnp.concatenate(diag_blocks, axis=0)

    q64, k64 = lvl(nl - 1)
    st = st_scr[head]
    o = _dot(a.astype(BF16), v) + _dot_nt(q64, st.astype(BF16))
    o_ref[0, rows, vs] = o.astype(o_ref.dtype)

    tot_row = (nl - 1) * c + (c - 1 if d == 0 else 0)
    st_scr[head] = st * e[tot_row:tot_row + 1] + _dot_tn(v, k64)


def _gla_kernel(qf_ref, kf_ref, vf_ref, lrf_ref, qb_ref, kb_ref, vb_ref, lrb_ref,
                w2f_ref, bf_ref, w2b_ref, bb_ref, tri_ref, msk_ref,
                of_ref, ob_ref, stf_scr, stb_scr, gf_scr, gb_scr):
    t = pl.program_id(1)

    @pl.when(t == 0)
    def _():
        stf_scr[...] = jnp.zeros_like(stf_scr)
        stb_scr[...] = jnp.zeros_like(stb_scr)

    inv_tau = 1.0 / GLA_TAU
    zf = _dot(lrf_ref[0].astype(BF16), w2f_ref[...]) + bf_ref[...]
    gf_scr[...] = _log_sigmoid(zf) * inv_tau
    zb = _dot(lrb_ref[0].astype(BF16), w2b_ref[...]) + bb_ref[...]
    gb_scr[...] = _log_sigmoid(zb) * inv_tau

    n_chunks = qf_ref.shape[1] // GLA_CHUNK

    def body(ci, carry):
        rf = pl.multiple_of(ci * GLA_CHUNK, GLA_CHUNK)
        rb = pl.multiple_of((n_chunks - 1 - ci) * GLA_CHUNK, GLA_CHUNK)
        for head in range(GLA_HEADS):
            _gla_chunk(0, head, rf, qf_ref, kf_ref, vf_ref, gf_scr, stf_scr, of_ref,
                       tri_ref, msk_ref)
            _gla_chunk(1, head, rb, qb_ref, kb_ref, vb_ref, gb_scr, stb_scr, ob_ref,
                       tri_ref, msk_ref)
        return carry

    lax.fori_loop(0, n_chunks, body, 0)


def _gla(proj3, glr3, w2f, bf, w2b, bb, tri, msk, tb):
    b, s, _ = proj3.shape
    nt = s // tb
    qk_w, v_w = GLA_HEADS * GLA_DK, GLA_HEADS * GLA_DV
    fwd = lambda col: (lambda bi, t: (bi, t, col))
    bwd = lambda col: (lambda bi, t: (bi, nt - 1 - t, col))
    const2 = lambda bi, t: (0, 0)
    in_specs = []
    for mk in (fwd, bwd):
        in_specs += [
            pl.BlockSpec((1, tb, qk_w), mk(0)),
            pl.BlockSpec((1, tb, qk_w), mk(1)),
            pl.BlockSpec((1, tb, v_w), mk(1)),
            pl.BlockSpec((1, tb, 128), mk(0)),
        ]
    in_specs += [
        pl.BlockSpec((128, qk_w), const2), pl.BlockSpec((1, qk_w), const2),
        pl.BlockSpec((128, qk_w), const2), pl.BlockSpec((1, qk_w), const2),
        pl.BlockSpec(tri.shape, lambda bi, t: (0, 0, 0)),
        pl.BlockSpec(msk.shape, lambda bi, t: (0, 0, 0, 0)),
    ]
    return pl.pallas_call(
        _gla_kernel,
        out_shape=(jax.ShapeDtypeStruct((b, s, v_w), BF16),
                   jax.ShapeDtypeStruct((b, s, v_w), BF16)),
        grid=(b, nt),
        in_specs=in_specs,
        out_specs=(pl.BlockSpec((1, tb, v_w), fwd(0)),
                   pl.BlockSpec((1, tb, v_w), bwd(0))),
        scratch_shapes=[pltpu.VMEM((GLA_HEADS, GLA_DV, GLA_DK), F32),
                        pltpu.VMEM((GLA_HEADS, GLA_DV, GLA_DK), F32),
                        pltpu.VMEM((tb, qk_w), F32),
                        pltpu.VMEM((tb, qk_w), F32)],
        compiler_params=pltpu.CompilerParams(
            dimension_semantics=("arbitrary", "arbitrary"),
            vmem_limit_bytes=VMEM_LIMIT_BYTES),
        name="gla",
    )(proj3, proj3, proj3, glr3, proj3, proj3, proj3, glr3, w2f, bf, w2b, bb, tri, msk)


def _na_bias_table(rpb):
    col = np.arange(GRID_W)
    cs = np.clip(col - NA_COLS // 2, 0, GRID_W - NA_COLS)
    inside = (col[None, :] >= cs[:, None]) & (col[None, :] < cs[:, None] + NA_COLS)
    dc = np.clip(col[None, :] - col[:, None] + (NA_COLS - 1), 0, 2 * NA_COLS - 2)
    t = jnp.where(inside[None, None], rpb[:, :, dc].astype(F32), NEG_BIG)
    return jnp.concatenate([t[:, :-1], t[:, 1:]], axis=-1)


def _na_kernel(q_ref, k_ref, v_ref, t2_ref, o_ref, *, n_rows):
    rb = q_ref.shape[1] // GRID_W
    t = pl.program_id(1)
    lane = lax.broadcasted_iota(jnp.int32, (GRID_W, 2 * NA_HD), 1)
    low = lane < NA_HD
    n_keys = NA_ROWS * GRID_W

    def body(rr, carry):
        r = t * rb + rr
        rs = jnp.clip(r - NA_ROWS // 2, 0, n_rows - NA_ROWS)
        off = rs - r + (NA_ROWS - 1)
        qrows = pl.ds(pl.multiple_of(rr * GRID_W, GRID_W), GRID_W)
        krows = pl.ds(pl.multiple_of(rs * GRID_W, GRID_W), n_keys)
        for p in range(NA_HEADS // 2):
            ls = slice(p * 2 * NA_HD, (p + 1) * 2 * NA_HD)
            q2 = q_ref[0, qrows, ls]
            k2 = k_ref[0, krows, ls]
            v2 = v_ref[0, krows, ls]
            outs = []
            for half in range(2):
                h = 2 * p + half
                qm = jnp.where(low if half == 0 else jnp.logical_not(low), q2, jnp.zeros_like(q2))
                bias = jnp.concatenate([t2_ref[h, off + 2 * w] for w in range(NA_ROWS // 2)], axis=1)
                s = _dot_nt(qm, k2) + bias
                m = jnp.max(s, axis=-1, keepdims=True)
                pe = jnp.exp(s - m)
                den = jnp.sum(pe, axis=-1, keepdims=True)
                outs.append(_dot(pe.astype(BF16), v2) / den)
            o_ref[0, qrows, ls] = jnp.where(low, outs[0], outs[1]).astype(o_ref.dtype)
        return carry

    lax.fori_loop(0, rb, body, 0)


def _na(proj3, t2, rb):
    b, s, _ = proj3.shape
    n_rows = s // GRID_W
    w = NA_HEADS * NA_HD
    return pl.pallas_call(
        functools.partial(_na_kernel, n_rows=n_rows),
        out_shape=jax.ShapeDtypeStruct((b, s, w), BF16),
        grid=(b, n_rows // rb),
        in_specs=[
            pl.BlockSpec((1, rb * GRID_W, w), lambda bi, t: (bi, t, T_NQ)),
            pl.BlockSpec((1, s, w), lambda bi, t: (bi, 0, T_NK)),
            pl.BlockSpec((1, s, w), lambda bi, t: (bi, 0, T_NV)),
            pl.BlockSpec(t2.shape, lambda bi, t: (0, 0, 0, 0)),
        ],
        out_specs=pl.BlockSpec((1, rb * GRID_W, w), lambda bi, t: (bi, t, 0)),
        compiler_params=pltpu.CompilerParams(
            dimension_semantics=("arbitrary", "arbitrary"),
            vmem_limit_bytes=VMEM_LIMIT_BYTES),
        name="na",
    )(proj3, proj3, proj3, t2)


def _mem_kv_kernel(mem_ref, g_ref, w_ref, kg_ref, k_ref, v_ref):
    xf = mem_ref[0]
    ms = jnp.mean(xf * xf, axis=-1, keepdims=True)
    h = (xf * lax.rsqrt(ms + EPS) * g_ref[...]).astype(BF16)
    kv = _dot(h, w_ref[...])
    wk = MEM_HEADS * MEM_HD
    parts = []
    for hd in range(MEM_HEADS):
        kh = kv[:, hd * MEM_HD:(hd + 1) * MEM_HD]
        msk = jnp.mean(kh * kh, axis=-1, keepdims=True)
        parts.append(kh * lax.rsqrt(msk + EPS) * kg_ref[...])
    k_ref[0] = jnp.concatenate(parts, axis=1).astype(BF16)
    v_ref[0] = kv[:, wk:].astype(BF16)


def _mem_kv(mem, g, w, kg):
    b, m, d = mem.shape
    wk = MEM_HEADS * MEM_HD
    return pl.pallas_call(
        _mem_kv_kernel,
        out_shape=(jax.ShapeDtypeStruct((b, m, wk), BF16),
                   jax.ShapeDtypeStruct((b, m, wk), BF16)),
        grid=(b,),
        in_specs=[
            pl.BlockSpec((1, m, d), lambda bi: (bi, 0, 0)),
            pl.BlockSpec((1, d), lambda bi: (0, 0)),
            pl.BlockSpec((d, 2 * wk), lambda bi: (0, 0)),
            pl.BlockSpec((1, MEM_HD), lambda bi: (0, 0)),
        ],
        out_specs=(pl.BlockSpec((1, m, wk), lambda bi: (bi, 0, 0)),
                   pl.BlockSpec((1, m, wk), lambda bi: (bi, 0, 0))),
        compiler_params=pltpu.CompilerParams(
            dimension_semantics=("arbitrary",),
            vmem_limit_bytes=VMEM_LIMIT_BYTES),
        name="mem_kv",
    )(mem, g, w, kg)


def _merge_kernel(of_ref, ob_ref, ggate_ref, nb_ref, ngate_ref, mq_ref, mgate_ref, sig_ref,
                  x_ref, kc_ref, vc_ref, gout_ref, pa_ref, pb_ref, pc_ref, wo_ref, out_ref):
    d = x_ref.shape[1]
    o = of_ref[...].astype(F32) + ob_ref[...].astype(F32)
    parts = []
    for h in range(GLA_HEADS):
        oh = o[:, h * GLA_DV:(h + 1) * GLA_DV]
        ms = jnp.mean(oh * oh, axis=-1, keepdims=True)
        parts.append(oh * lax.rsqrt(ms + EPS) * gout_ref[...])
    oa = jnp.concatenate(parts, axis=1) * ggate_ref[...].astype(F32)
    ya = _dot(oa.astype(BF16), pa_ref[...])

    nb = nb_ref[...].astype(F32) * ngate_ref[...].astype(F32)
    yb = _dot(nb.astype(BF16), pb_ref[...])

    parts = []
    for h in range(MEM_HEADS):
        hs = slice(h * MEM_HD, (h + 1) * MEM_HD)
        s = _dot_nt(mq_ref[:, hs], kc_ref[0, :, hs])
        m = jnp.max(s, axis=-1, keepdims=True)
        pe = jnp.exp(s - m)
        den = jnp.sum(pe, axis=-1, keepdims=True)
        parts.append(_dot(pe.astype(BF16), vc_ref[0, :, hs]) / den)
    oc = jnp.concatenate(parts, axis=1) * mgate_ref[...].astype(F32)
    yc = _dot(oc.astype(BF16), pc_ref[...])

    y = (sig_ref[:, 0:d].astype(F32) * ya + sig_ref[:, d:2 * d].astype(F32) * yb
         + sig_ref[:, 2 * d:3 * d].astype(F32) * yc)
    out_ref[...] = x_ref[...] + _dot(y.astype(BF16), wo_ref[...])


def _merge(of2, ob2, nb2, proj2, x2, kc, vc, gout, pa, pb, pc, wo, tm, seq):
    m, d = x2.shape
    v_w = GLA_HEADS * GLA_DV
    na_w = NA_HEADS * NA_HD
    mem_w = MEM_HEADS * MEM_HD
    steps_per_batch = seq // tm
    row = lambda col: (lambda i: (i, col))
    const = lambda i: (0, 0)
    batch = lambda i: (i // steps_per_batch, 0, 0)
    return pl.pallas_call(
        _merge_kernel,
        out_shape=jax.ShapeDtypeStruct((m, d), F32),
        grid=(m // tm,),
        in_specs=[
            pl.BlockSpec((tm, v_w), row(0)),
            pl.BlockSpec((tm, v_w), row(0)),
            pl.BlockSpec((tm, v_w), row(T_GGATE * COL_TILE // v_w)),
            pl.BlockSpec((tm, na_w), row(0)),
            pl.BlockSpec((tm, na_w), row(T_NGATE)),
            pl.BlockSpec((tm, mem_w), row(T_MQ)),
            pl.BlockSpec((tm, mem_w), row(T_MGATE)),
            pl.BlockSpec((tm, 3 * d), row(T_MERGE * COL_TILE // (3 * d))),
            pl.BlockSpec((tm, d), row(0)),
            pl.BlockSpec((1,) + kc.shape[1:], batch),
            pl.BlockSpec((1,) + vc.shape[1:], batch),
            pl.BlockSpec((1, GLA_DV), const),
            pl.BlockSpec(pa.shape, const),
            pl.BlockSpec(pb.shape, const),
            pl.BlockSpec(pc.shape, const),
            pl.BlockSpec(wo.shape, const),
        ],
        out_specs=pl.BlockSpec((tm, d), row(0)),
        compiler_params=pltpu.CompilerParams(
            dimension_semantics=("arbitrary",),
            vmem_limit_bytes=VMEM_LIMIT_BYTES),
        name="merge",
    )(of2, ob2, proj2, nb2, proj2, proj2, proj2, proj2, x2, kc, vc, gout, pa, pb, pc, wo)


def _block_diag_mean(width, group):
    i = np.arange(width) // group
    return jnp.asarray((i[:, None] == i[None, :]) / group, BF16)


def _layer_params(l, norm_g, w_in, gla_w2_f, gla_b_f, gla_w2_b, gla_b_b, gla_out_g, p_a,
                  na_q_g, na_k_g, na_rpb, p_b, mem_norm_g, w_mem_kv, mem_q_g, mem_k_g, p_c,
                  w_out):
    d = w_in.shape[1]
    qk_w = GLA_HEADS * GLA_DK
    lr0 = 2 * qk_w + 2 * GLA_HEADS * GLA_DV
    lr1 = lr0 + 2 * GLA_RANK
    w = w_in[l]
    w_main = jnp.concatenate([w[:, :lr0], w[:, lr1:]], axis=1).astype(BF16)
    w_lr = jnp.pad(w[:, lr0:lr1], ((0, 0), (0, 128 - 2 * GLA_RANK))).astype(BF16)

    gains = jnp.ones((N_COL_TILES, COL_TILE), F32)
    gains = gains.at[T_GQ].set(GLA_DK ** -0.5)
    gains = gains.at[T_NQ].set(jnp.tile(na_q_g[l].astype(F32) * NA_HD ** -0.5, NA_HEADS))
    gains = gains.at[T_NK].set(jnp.tile(na_k_g[l].astype(F32), NA_HEADS))
    gains = gains.at[T_MQ].set(jnp.tile(mem_q_g[l].astype(F32) * MEM_HD ** -0.5, MEM_HEADS))

    w2f = jnp.zeros((128, qk_w), F32).at[:GLA_RANK].set(gla_w2_f[l]).astype(BF16)
    w2b = jnp.zeros((128, qk_w), F32).at[GLA_RANK:2 * GLA_RANK].set(gla_w2_b[l]).astype(BF16)
    return dict(
        norm_g=norm_g[l].reshape(1, d).astype(F32),
        w_main=w_main, w_lr=w_lr, gains=gains.reshape(N_COL_TILES, 1, COL_TILE),
        w2f=w2f, bf=gla_b_f[l].reshape(1, qk_w).astype(F32),
        w2b=w2b, bb=gla_b_b[l].reshape(1, qk_w).astype(F32),
        gout=gla_out_g[l].reshape(1, GLA_DV).astype(F32),
        t2=_na_bias_table(na_rpb[l]),
        mem_g=mem_norm_g[l].reshape(1, d).astype(F32),
        w_kv=w_mem_kv[l].astype(BF16),
        mem_kg=mem_k_g[l].reshape(1, MEM_HD).astype(F32),
        pa=p_a[l].astype(BF16), pb=p_b[l].astype(BF16), pc=p_c[l].astype(BF16),
        wo=w_out[l].astype(BF16),
    )


def _tiles(batch, seq):
    m = batch * seq
    tm_in = 1024 if m % 1024 == 0 else seq
    tm_merge = 512 if seq % 512 == 0 else seq
    tb_gla = 512 if seq % 512 == 0 else seq
    rb_na = 8
    return tm_in, tm_merge, tb_gla, rb_na


def kernel(x, mem, norm_g, w_in, gla_w2_f, gla_b_f, gla_w2_b, gla_b_b, gla_out_g, p_a, na_q_g,
           na_k_g, na_rpb, p_b, mem_norm_g, w_mem_kv, mem_q_g, mem_k_g, p_c, w_out):
    b, s, d = x.shape
    depth = w_in.shape[0]
    n_rows = s // GRID_W
    assert s % GRID_W == 0 and n_rows >= NA_ROWS and s % GLA_CHUNK == 0
    assert w_in.shape[2] - 2 * GLA_RANK == N_MAIN
    tm_in, tm_merge, tb_gla, rb_na = _tiles(b, s)
    tri, msk = _gla_constants()
    bd64 = _block_diag_mean(COL_TILE, NA_HD)
    bd128 = _block_diag_mean(COL_TILE, MEM_HD)

    x2 = x.reshape(b * s, d).astype(F32)
    for l in range(depth):
        p = _layer_params(l, norm_g, w_in, gla_w2_f, gla_b_f, gla_w2_b, gla_b_b, gla_out_g,
                          p_a, na_q_g, na_k_g, na_rpb, p_b, mem_norm_g, w_mem_kv, mem_q_g,
                          mem_k_g, p_c, w_out)
        proj2, glr2 = _in_proj(x2, p["norm_g"], p["w_main"], p["w_lr"], p["gains"], bd64,
                               bd128, tm_in)
        proj3 = proj2.reshape(b, s, N_MAIN)
        glr3 = glr2.reshape(b, s, 128)
        o_f, o_b = _gla(proj3, glr3, p["w2f"], p["bf"], p["w2b"], p["bb"], tri, msk, tb_gla)
        nb = _na(proj3, p["t2"], rb_na)
        kc, vc = _mem_kv(mem.astype(F32), p["mem_g"], p["w_kv"], p["mem_kg"])
        x2 = _merge(o_f.reshape(b * s, -1), o_b.reshape(b * s, -1), nb.reshape(b * s, -1),
                    proj2, x2, kc, vc, p["gout"], p["pa"], p["pb"], p["pc"], p["wo"],
                    tm_merge, s)
    return x2.reshape(b, s, d).astype(x.dtype)
```

```python
import functools

import numpy as np
import jax
import jax.numpy as jnp
from jax import lax
from jax.experimental import pallas as pl
from jax.experimental.pallas import tpu as pltpu

F32 = jnp.float32
BF16 = jnp.bfloat16

EPS = 1e-6
NEG_BIG = -1e30
LOG2_E = 1.4426950408889634

GRID_W = 64
GLA_HEADS, GLA_DK, GLA_DV = 4, 128, 256
GLA_RANK = 16
GLA_TAU = 16.0
GLA_CHUNK = 64
GLA_LEVELS = (8, 16, 32, 64)
GLA_PHASES = 4
NA_HEADS, NA_HD = 8, 64
NA_ROWS, NA_COLS = 8, 16
NA_GROUP = 4
MEM_HEADS, MEM_HD = 4, 128

COL_TILE = 512
N_MAIN = 9216
T_GQ, T_GK, T_GV, T_GGATE = 0, 1, 2, 4
T_NQ, T_NK, T_NV, T_NGATE = 6, 7, 8, 9
T_MQ, T_MGATE, T_MERGE = 10, 11, 12
N_COL_TILES = N_MAIN // COL_TILE
IN_PROJ_SUB_ROWS = 256

VMEM_LIMIT_BYTES = 48 * 1024 * 1024


def _dot(a, b):
    return jnp.dot(a, b, preferred_element_type=F32)


def _dot_nt(a, b):
    return lax.dot_general(a, b, (((1,), (1,)), ((), ())), preferred_element_type=F32)


def _dot_tn(a, b):
    return lax.dot_general(a, b, (((0,), (0,)), ((), ())), preferred_element_type=F32)


def _in_proj_kernel(x_ref, g_ref, w_ref, wlr_ref, gain_ref, bd64_ref, bd128_ref,
                    proj_ref, glr_ref, h_ref):
    j = pl.program_id(1)

    @pl.when(j == 0)
    def _():
        xf = x_ref[...]
        ms = jnp.mean(xf * xf, axis=-1, keepdims=True)
        h = (xf * lax.rsqrt(ms + EPS) * g_ref[...]).astype(BF16)
        h_ref[...] = h
        glr_ref[...] = _dot(h, wlr_ref[...])

    gain = gain_ref[0]

    def tile(epilogue):
        for r0 in range(0, h_ref.shape[0], IN_PROJ_SUB_ROWS):
            rows = slice(r0, r0 + IN_PROJ_SUB_ROWS)
            acc = _dot(h_ref[rows, :], w_ref[...])
            proj_ref[rows, :] = epilogue(acc).astype(BF16)

    def normed(bd_ref):
        def epilogue(acc):
            ms = _dot((acc * acc).astype(BF16), bd_ref[...])
            return acc * lax.rsqrt(ms + EPS) * gain
        return epilogue

    is_silu = (j == T_GGATE) | (j == T_GGATE + 1) | (j == T_NGATE) | (j == T_MGATE)
    is_sig = j >= T_MERGE
    is_n64 = (j == T_NQ) | (j == T_NK)
    is_n128 = j == T_MQ
    is_lin = jnp.logical_not(is_silu | is_sig | is_n64 | is_n128)

    pl.when(is_lin)(lambda: tile(lambda acc: acc * gain))
    pl.when(is_silu)(lambda: tile(lambda acc: acc * jax.nn.sigmoid(acc)))
    pl.when(is_sig)(lambda: tile(jax.nn.sigmoid))
    pl.when(is_n64)(lambda: tile(normed(bd64_ref)))
    pl.when(is_n128)(lambda: tile(normed(bd128_ref)))


def _in_proj(x2, g, w_main, w_lr, gains, bd64, bd128, tm):
    m, d = x2.shape
    return pl.pallas_call(
        _in_proj_kernel,
        out_shape=(jax.ShapeDtypeStruct((m, N_MAIN), BF16),
                   jax.ShapeDtypeStruct((m, 128), F32)),
        grid=(m // tm, N_COL_TILES),
        in_specs=[
            pl.BlockSpec((tm, d), lambda i, j: (i, 0)),
            pl.BlockSpec((1, d), lambda i, j: (0, 0)),
            pl.BlockSpec((d, COL_TILE), lambda i, j: (0, j)),
            pl.BlockSpec((d, 128), lambda i, j: (0, 0)),
            pl.BlockSpec((1, 1, COL_TILE), lambda i, j: (j, 0, 0)),
            pl.BlockSpec((COL_TILE, COL_TILE), lambda i, j: (0, 0)),
            pl.BlockSpec((COL_TILE, COL_TILE), lambda i, j: (0, 0)),
        ],
        out_specs=(pl.BlockSpec((tm, COL_TILE), lambda i, j: (i, j)),
                   pl.BlockSpec((tm, 128), lambda i, j: (i, 0))),
        scratch_shapes=[pltpu.VMEM((tm, d), BF16)],
        compiler_params=pltpu.CompilerParams(
            dimension_semantics=("arbitrary", "arbitrary"),
            vmem_limit_bytes=VMEM_LIMIT_BYTES),
        name="in_proj",
    )(x2, g, w_main, w_lr, gains, bd64, bd128)


def _gla_constants():
    c = GLA_CHUNK
    idx = np.arange(c)
    tri = np.zeros((2, 2 * len(GLA_LEVELS) * c, c), np.float32)
    msk = np.zeros((2, len(GLA_LEVELS), c, c), np.float32)
    for d in range(2):
        covered = np.zeros((c, c), bool)
        for n, l in enumerate(GLA_LEVELS):
            blk = idx // l
            same = blk[:, None] == blk[None, :]
            if d == 0:
                wq = same & (idx[None, :] <= idx[:, None])
                wk = same & (idx[None, :] > idx[:, None])
                adj = blk[:, None] == blk[None, :] + 1
            else:
                wq = same & (idx[None, :] >= idx[:, None])
                wk = same & (idx[None, :] < idx[:, None])
                adj = blk[None, :] == blk[:, None] + 1
            tri[d, n * c:(n + 1) * c] = wq
            tri[d, (len(GLA_LEVELS) + n) * c:(len(GLA_LEVELS) + n + 1) * c] = wk
            if l != c:
                m = adj & ~covered
                covered |= m
                msk[d, n] = m
    tri = np.concatenate([tri, tri], axis=2)
    return jnp.asarray(tri, BF16), jnp.asarray(msk[:, :len(GLA_LEVELS) - 1], F32)


def _log_sigmoid(z):
    return jnp.minimum(z, 0.0) - jnp.log(1.0 + jnp.exp(-jnp.abs(z)))


def _gla_chunk(d, head, r0, q_ref, k_ref, v_ref, g_scr, st_scr, o_ref, tri_ref, msk_ref):
    c, nl = GLA_CHUNK, len(GLA_LEVELS)
    ks = slice(head * GLA_DK, (head + 1) * GLA_DK)
    vs = slice(head * GLA_DV, (head + 1) * GLA_DV)
    rows = pl.ds(r0, c)
    q = q_ref[0, rows, ks].astype(F32)
    k = k_ref[0, rows, ks].astype(F32)
    v = v_ref[0, rows, vs]
    g = g_scr[rows, ks]

    g_hi = g.astype(BF16)
    g_lo = (g - g_hi.astype(F32)).astype(BF16)
    sums = _dot(tri_ref[d], jnp.concatenate([g_hi, g_lo], axis=0))
    yield
    e = jnp.exp2(sums)

    def lvl(n):
        eq = e[n * c:(n + 1) * c]
        ek = e[(nl + n) * c:(nl + n + 1) * c]
        return (q * eq).astype(BF16), (k * ek).astype(BF16)

    lvl_dots = []
    for n in range(nl - 1):
        qn, kn = lvl(n)
        lvl_dots.append(_dot_nt(qn, kn))
    q64, k64 = lvl(nl - 1)
    st = st_scr[head]
    o_inter = _dot_nt(q64, st.astype(BF16))
    st_update = _dot_tn(v, k64)
    yield

    w8 = sums[0:c]
    row = lax.broadcasted_iota(jnp.int32, (8, GLA_DK), 0)
    lane = lax.broadcasted_iota(jnp.int32, (8, c), 1)
    diag_blocks = []
    for p in range(c // 8):
        wp, qp, kp = w8[8 * p:8 * p + 8], q[8 * p:8 * p + 8], k[8 * p:8 * p + 8]
        ad = jnp.zeros((8, c), F32)
        for jj in range(8):
            valid = (row >= jj) if d == 0 else (row <= jj)
            ex = jnp.exp2(jnp.where(valid, wp - wp[jj:jj + 1], NEG_BIG))
            col = jnp.sum(qp * ex * kp[jj:jj + 1], axis=-1, keepdims=True)
            ad = jnp.where(lane == 8 * p + jj, col, ad)
        diag_blocks.append(ad)
    a = jnp.concatenate(diag_blocks, axis=0)
    yield

    for n in range(nl - 1):
        a = a + msk_ref[d, n] * lvl_dots[n]
    o = _dot(a.astype(BF16), v) + o_inter
    o_ref[0, rows, vs] = o.astype(o_ref.dtype)

    tot_row = (nl - 1) * c + (c - 1 if d == 0 else 0)
    st_scr[head] = st * e[tot_row:tot_row + 1] + st_update
    yield


def _gla_kernel(qf_ref, kf_ref, vf_ref, lrf_ref, qb_ref, kb_ref, vb_ref, lrb_ref,
                w2f_ref, bf_ref, w2b_ref, bb_ref, tri_ref, msk_ref,
                of_ref, ob_ref, stf_scr, stb_scr, gf_scr, gb_scr):
    t = pl.program_id(1)

    @pl.when(t == 0)
    def _():
        stf_scr[...] = jnp.zeros_like(stf_scr)
        stb_scr[...] = jnp.zeros_like(stb_scr)

    to_log2_decay = LOG2_E / GLA_TAU
    zf = _dot(lrf_ref[0].astype(BF16), w2f_ref[...]) + bf_ref[...]
    gf_scr[...] = _log_sigmoid(zf) * to_log2_decay
    zb = _dot(lrb_ref[0].astype(BF16), w2b_ref[...]) + bb_ref[...]
    gb_scr[...] = _log_sigmoid(zb) * to_log2_decay

    n_chunks = qf_ref.shape[1] // GLA_CHUNK

    def body(ci, carry):
        rf = pl.multiple_of(ci * GLA_CHUNK, GLA_CHUNK)
        rb = pl.multiple_of((n_chunks - 1 - ci) * GLA_CHUNK, GLA_CHUNK)
        streams = []
        for head in range(GLA_HEADS):
            streams.append(_gla_chunk(0, head, rf, qf_ref, kf_ref, vf_ref, gf_scr, stf_scr,
                                      of_ref, tri_ref, msk_ref))
            streams.append(_gla_chunk(1, head, rb, qb_ref, kb_ref, vb_ref, gb_scr, stb_scr,
                                      ob_ref, tri_ref, msk_ref))
        for _ in range(GLA_PHASES):
            for stream in streams:
                next(stream)
        return carry

    lax.fori_loop(0, n_chunks, body, 0)


def _gla(proj3, glr3, w2f, bf, w2b, bb, tri, msk, tb):
    b, s, _ = proj3.shape
    nt = s // tb
    qk_w, v_w = GLA_HEADS * GLA_DK, GLA_HEADS * GLA_DV
    fwd = lambda col: (lambda bi, t: (bi, t, col))
    bwd = lambda col: (lambda bi, t: (bi, nt - 1 - t, col))
    const2 = lambda bi, t: (0, 0)
    in_specs = []
    for mk in (fwd, bwd):
        in_specs += [
            pl.BlockSpec((1, tb, qk_w), mk(0)),
            pl.BlockSpec((1, tb, qk_w), mk(1)),
            pl.BlockSpec((1, tb, v_w), mk(1)),
            pl.BlockSpec((1, tb, 128), mk(0)),
        ]
    in_specs += [
        pl.BlockSpec((128, qk_w), const2), pl.BlockSpec((1, qk_w), const2),
        pl.BlockSpec((128, qk_w), const2), pl.BlockSpec((1, qk_w), const2),
        pl.BlockSpec(tri.shape, lambda bi, t: (0, 0, 0)),
        pl.BlockSpec(msk.shape, lambda bi, t: (0, 0, 0, 0)),
    ]
    return pl.pallas_call(
        _gla_kernel,
        out_shape=(jax.ShapeDtypeStruct((b, s, v_w), BF16),
                   jax.ShapeDtypeStruct((b, s, v_w), BF16)),
        grid=(b, nt),
        in_specs=in_specs,
        out_specs=(pl.BlockSpec((1, tb, v_w), fwd(0)),
                   pl.BlockSpec((1, tb, v_w), bwd(0))),
        scratch_shapes=[pltpu.VMEM((GLA_HEADS, GLA_DV, GLA_DK), F32),
                        pltpu.VMEM((GLA_HEADS, GLA_DV, GLA_DK), F32),
                        pltpu.VMEM((tb, qk_w), F32),
                        pltpu.VMEM((tb, qk_w), F32)],
        compiler_params=pltpu.CompilerParams(
            dimension_semantics=("arbitrary", "arbitrary"),
            vmem_limit_bytes=VMEM_LIMIT_BYTES),
        name="gla",
    )(proj3, proj3, proj3, glr3, proj3, proj3, proj3, glr3, w2f, bf, w2b, bb, tri, msk)


def _na_bias_table(rpb):
    col = np.arange(GRID_W)
    cs = np.clip(col - NA_COLS // 2, 0, GRID_W - NA_COLS)
    inside = (col[None, :] >= cs[:, None]) & (col[None, :] < cs[:, None] + NA_COLS)
    dc = col[None, :] - col[:, None] + (NA_COLS - 1)
    onehot = (np.arange(2 * NA_COLS - 1)[:, None, None] == dc[None]) & inside[None]
    t = jnp.einsum("hdc,cqk->hdqk", rpb.astype(F32), jnp.asarray(onehot, F32),
                   precision=lax.Precision.HIGHEST)
    t = jnp.where(inside[None, None], t, NEG_BIG)
    return jnp.concatenate([t[:, :-1], t[:, 1:]], axis=-1)


def _na_kernel(q_ref, k_ref, v_ref, t2_ref, o_ref, *, n_rows):
    rb = q_ref.shape[1] // GRID_W
    t = pl.program_id(1)
    gw = NA_GROUP * NA_HD
    head_of_lane = lax.broadcasted_iota(jnp.int32, (GRID_W, gw), 1) // NA_HD
    n_keys = NA_ROWS * GRID_W

    def pick(blocks):
        out = blocks[(NA_GROUP - 1) * GRID_W:]
        for h in range(NA_GROUP - 2, -1, -1):
            out = jnp.where(head_of_lane == h, blocks[h * GRID_W:(h + 1) * GRID_W], out)
        return out

    def body(rr, carry):
        r = t * rb + rr
        rs = jnp.clip(r - NA_ROWS // 2, 0, n_rows - NA_ROWS)
        off = rs - r + (NA_ROWS - 1)
        qrows = pl.ds(pl.multiple_of(rr * GRID_W, GRID_W), GRID_W)
        krows = pl.ds(pl.multiple_of(rs * GRID_W, GRID_W), n_keys)
        for grp in range(NA_HEADS // NA_GROUP):
            ls = slice(grp * gw, (grp + 1) * gw)
            q4 = q_ref[0, qrows, ls]
            k4 = k_ref[0, krows, ls]
            v4 = v_ref[0, krows, ls]
            lhs = jnp.concatenate(
                [jnp.where(head_of_lane == h, q4, jnp.zeros_like(q4)) for h in range(NA_GROUP)],
                axis=0)
            bias = jnp.concatenate(
                [jnp.concatenate([t2_ref[grp * NA_GROUP + h, off + 2 * w]
                                  for w in range(NA_ROWS // 2)], axis=1)
                 for h in range(NA_GROUP)], axis=0)
            s = _dot_nt(lhs, k4) + bias
            m = jnp.max(s, axis=-1, keepdims=True)
            pe = jnp.exp(s - m)
            den = jnp.sum(pe, axis=-1, keepdims=True)
            pv = _dot(pe.astype(BF16), v4)
            o_ref[0, qrows, ls] = (pick(pv) / pick(den)).astype(o_ref.dtype)
        return carry

    lax.fori_loop(0, rb, body, 0, unroll=2)


def _na(proj3, t2, rb):
    b, s, _ = proj3.shape
    n_rows = s // GRID_W
    w = NA_HEADS * NA_HD
    return pl.pallas_call(
        functools.partial(_na_kernel, n_rows=n_rows),
        out_shape=jax.ShapeDtypeStruct((b, s, w), BF16),
        grid=(b, n_rows // rb),
        in_specs=[
            pl.BlockSpec((1, rb * GRID_W, w), lambda bi, t: (bi, t, T_NQ)),
            pl.BlockSpec((1, s, w), lambda bi, t: (bi, 0, T_NK)),
            pl.BlockSpec((1, s, w), lambda bi, t: (bi, 0, T_NV)),
            pl.BlockSpec(t2.shape, lambda bi, t: (0, 0, 0, 0)),
        ],
        out_specs=pl.BlockSpec((1, rb * GRID_W, w), lambda bi, t: (bi, t, 0)),
        compiler_params=pltpu.CompilerParams(
            dimension_semantics=("arbitrary", "arbitrary"),
            vmem_limit_bytes=VMEM_LIMIT_BYTES),
        name="na",
    )(proj3, proj3, proj3, t2)


def _mem_kv_kernel(mem_ref, g_ref, w_ref, kg_ref, k_ref, v_ref):
    xf = mem_ref[0]
    ms = jnp.mean(xf * xf, axis=-1, keepdims=True)
    h = (xf * lax.rsqrt(ms + EPS) * g_ref[...]).astype(BF16)
    kv = _dot(h, w_ref[...])
    wk = MEM_HEADS * MEM_HD
    parts = []
    for hd in range(MEM_HEADS):
        kh = kv[:, hd * MEM_HD:(hd + 1) * MEM_HD]
        msk = jnp.mean(kh * kh, axis=-1, keepdims=True)
        parts.append(kh * lax.rsqrt(msk + EPS) * kg_ref[...])
    k_ref[0] = jnp.concatenate(parts, axis=1).astype(BF16)
    v_ref[0] = kv[:, wk:].astype(BF16)


def _mem_kv(mem, g, w, kg):
    b, m, d = mem.shape
    wk = MEM_HEADS * MEM_HD
    return pl.pallas_call(
        _mem_kv_kernel,
        out_shape=(jax.ShapeDtypeStruct((b, m, wk), BF16),
                   jax.ShapeDtypeStruct((b, m, wk), BF16)),
        grid=(b,),
        in_specs=[
            pl.BlockSpec((1, m, d), lambda bi: (bi, 0, 0)),
            pl.BlockSpec((1, d), lambda bi: (0, 0)),
            pl.BlockSpec((d, 2 * wk), lambda bi: (0, 0)),
            pl.BlockSpec((1, MEM_HD), lambda bi: (0, 0)),
        ],
        out_specs=(pl.BlockSpec((1, m, wk), lambda bi: (bi, 0, 0)),
                   pl.BlockSpec((1, m, wk), lambda bi: (bi, 0, 0))),
        compiler_params=pltpu.CompilerParams(
            dimension_semantics=("arbitrary",),
            vmem_limit_bytes=VMEM_LIMIT_BYTES),
        name="mem_kv",
    )(mem, g, w, kg)


def _merge_kernel(of_ref, ob_ref, ggate_ref, nb_ref, ngate_ref, mq_ref, mgate_ref, sig_ref,
                  x_ref, kc_ref, vc_ref, gout_ref, pa_ref, pb_ref, pc_ref, wo_ref, out_ref):
    d = x_ref.shape[1]
    o = of_ref[...].astype(F32) + ob_ref[...].astype(F32)
    parts = []
    for h in range(GLA_HEADS):
        oh = o[:, h * GLA_DV:(h + 1) * GLA_DV]
        ms = jnp.mean(oh * oh, axis=-1, keepdims=True)
        parts.append(oh * lax.rsqrt(ms + EPS) * gout_ref[...])
    oa = jnp.concatenate(parts, axis=1) * ggate_ref[...].astype(F32)
    ya = _dot(oa.astype(BF16), pa_ref[...])

    nb = nb_ref[...].astype(F32) * ngate_ref[...].astype(F32)
    yb = _dot(nb.astype(BF16), pb_ref[...])

    parts = []
    for h in range(MEM_HEADS):
        hs = slice(h * MEM_HD, (h + 1) * MEM_HD)
        s = _dot_nt(mq_ref[:, hs], kc_ref[0, :, hs])
        m = jnp.max(s, axis=-1, keepdims=True)
        pe = jnp.exp(s - m)
        den = jnp.sum(pe, axis=-1, keepdims=True)
        parts.append(_dot(pe.astype(BF16), vc_ref[0, :, hs]) / den)
    oc = jnp.concatenate(parts, axis=1) * mgate_ref[...].astype(F32)
    yc = _dot(oc.astype(BF16), pc_ref[...])

    y = (sig_ref[:, 0:d].astype(F32) * ya + sig_ref[:, d:2 * d].astype(F32) * yb
         + sig_ref[:, 2 * d:3 * d].astype(F32) * yc)
    out_ref[...] = x_ref[...] + _dot(y.astype(BF16), wo_ref[...])


def _merge(of2, ob2, nb2, proj2, x2, kc, vc, gout, pa, pb, pc, wo, tm, seq):
    m, d = x2.shape
    v_w = GLA_HEADS * GLA_DV
    na_w = NA_HEADS * NA_HD
    mem_w = MEM_HEADS * MEM_HD
    steps_per_batch = seq // tm
    row = lambda col: (lambda i: (i, col))
    const = lambda i: (0, 0)
    batch = lambda i: (i // steps_per_batch, 0, 0)
    return pl.pallas_call(
        _merge_kernel,
        out_shape=jax.ShapeDtypeStruct((m, d), F32),
        grid=(m // tm,),
        in_specs=[
            pl.BlockSpec((tm, v_w), row(0)),
            pl.BlockSpec((tm, v_w), row(0)),
            pl.BlockSpec((tm, v_w), row(T_GGATE * COL_TILE // v_w)),
            pl.BlockSpec((tm, na_w), row(0)),
            pl.BlockSpec((tm, na_w), row(T_NGATE)),
            pl.BlockSpec((tm, mem_w), row(T_MQ)),
            pl.BlockSpec((tm, mem_w), row(T_MGATE)),
            pl.BlockSpec((tm, 3 * d), row(T_MERGE * COL_TILE // (3 * d))),
            pl.BlockSpec((tm, d), row(0)),
            pl.BlockSpec((1,) + kc.shape[1:], batch),
            pl.BlockSpec((1,) + vc.shape[1:], batch),
            pl.BlockSpec((1, GLA_DV), const),
            pl.BlockSpec(pa.shape, const),
            pl.BlockSpec(pb.shape, const),
            pl.BlockSpec(pc.shape, const),
            pl.BlockSpec(wo.shape, const),
        ],
        out_specs=pl.BlockSpec((tm, d), row(0)),
        compiler_params=pltpu.CompilerParams(
            dimension_semantics=("arbitrary",),
            vmem_limit_bytes=VMEM_LIMIT_BYTES),
        name="merge",
    )(of2, ob2, proj2, nb2, proj2, proj2, proj2, proj2, x2, kc, vc, gout, pa, pb, pc, wo)


def _block_diag_mean(width, group):
    i = np.arange(width) // group
    return jnp.asarray((i[:, None] == i[None, :]) / group, BF16)


def _layer_params(l, norm_g, w_in, gla_w2_f, gla_b_f, gla_w2_b, gla_b_b, gla_out_g, p_a,
                  na_q_g, na_k_g, na_rpb, p_b, mem_norm_g, w_mem_kv, mem_q_g, mem_k_g, p_c,
                  w_out):
    d = w_in.shape[1]
    qk_w = GLA_HEADS * GLA_DK
    lr0 = 2 * qk_w + 2 * GLA_HEADS * GLA_DV
    lr1 = lr0 + 2 * GLA_RANK
    w = w_in[l]
    w_main = jnp.concatenate([w[:, :lr0], w[:, lr1:]], axis=1).astype(BF16)
    w_lr = jnp.pad(w[:, lr0:lr1], ((0, 0), (0, 128 - 2 * GLA_RANK))).astype(BF16)

    gains = jnp.ones((N_COL_TILES, COL_TILE), F32)
    gains = gains.at[T_GQ].set(GLA_DK ** -0.5)
    gains = gains.at[T_NQ].set(jnp.tile(na_q_g[l].astype(F32) * NA_HD ** -0.5, NA_HEADS))
    gains = gains.at[T_NK].set(jnp.tile(na_k_g[l].astype(F32), NA_HEADS))
    gains = gains.at[T_MQ].set(jnp.tile(mem_q_g[l].astype(F32) * MEM_HD ** -0.5, MEM_HEADS))

    w2f = jnp.zeros((128, qk_w), F32).at[:GLA_RANK].set(gla_w2_f[l]).astype(BF16)
    w2b = jnp.zeros((128, qk_w), F32).at[GLA_RANK:2 * GLA_RANK].set(gla_w2_b[l]).astype(BF16)
    return dict(
        norm_g=norm_g[l].reshape(1, d).astype(F32),
        w_main=w_main, w_lr=w_lr, gains=gains.reshape(N_COL_TILES, 1, COL_TILE),
        w2f=w2f, bf=gla_b_f[l].reshape(1, qk_w).astype(F32),
        w2b=w2b, bb=gla_b_b[l].reshape(1, qk_w).astype(F32),
        gout=gla_out_g[l].reshape(1, GLA_DV).astype(F32),
        t2=_na_bias_table(na_rpb[l]),
        mem_g=mem_norm_g[l].reshape(1, d).astype(F32),
        w_kv=w_mem_kv[l].astype(BF16),
        mem_kg=mem_k_g[l].reshape(1, MEM_HD).astype(F32),
        pa=p_a[l].astype(BF16), pb=p_b[l].astype(BF16), pc=p_c[l].astype(BF16),
        wo=w_out[l].astype(BF16),
    )


def _tiles(batch, seq):
    m = batch * seq
    tm_in = 1024 if m % 1024 == 0 else seq
    tm_merge = 512 if seq % 512 == 0 else seq
    tb_gla = 512 if seq % 512 == 0 else seq
    rb_na = 8
    return tm_in, tm_merge, tb_gla, rb_na


def kernel(x, mem, norm_g, w_in, gla_w2_f, gla_b_f, gla_w2_b, gla_b_b, gla_out_g, p_a, na_q_g,
           na_k_g, na_rpb, p_b, mem_norm_g, w_mem_kv, mem_q_g, mem_k_g, p_c, w_out):
    b, s, d = x.shape
    depth = w_in.shape[0]
    n_rows = s // GRID_W
    assert s % GRID_W == 0 and n_rows >= NA_ROWS and s % GLA_CHUNK == 0
    assert w_in.shape[2] - 2 * GLA_RANK == N_MAIN
    tm_in, tm_merge, tb_gla, rb_na = _tiles(b, s)
    tri, msk = _gla_constants()
    bd64 = _block_diag_mean(COL_TILE, NA_HD)
    bd128 = _block_diag_mean(COL_TILE, MEM_HD)

    x2 = x.reshape(b * s, d).astype(F32)
    for l in range(depth):
        p = _layer_params(l, norm_g, w_in, gla_w2_f, gla_b_f, gla_w2_b, gla_b_b, gla_out_g,
                          p_a, na_q_g, na_k_g, na_rpb, p_b, mem_norm_g, w_mem_kv, mem_q_g,
                          mem_k_g, p_c, w_out)
        proj2, glr2 = _in_proj(x2, p["norm_g"], p["w_main"], p["w_lr"], p["gains"], bd64,
                               bd128, tm_in)
        proj3 = proj2.reshape(b, s, N_MAIN)
        glr3 = glr2.reshape(b, s, 128)
        o_f, o_b = _gla(proj3, glr3, p["w2f"], p["bf"], p["w2b"], p["bb"], tri, msk, tb_gla)
        nb = _na(proj3, p["t2"], rb_na)
        kc, vc = _mem_kv(mem.astype(F32), p["mem_g"], p["w_kv"], p["mem_kg"])
        x2 = _merge(o_f.reshape(b * s, -1), o_b.reshape(b * s, -1), nb.reshape(b * s, -1),
                    proj2, x2, kc, vc, p["gout"], p["pa"], p["pb"], p["pc"], p["wo"],
                    tm_merge, s)
    return x2.reshape(b, s, d).astype(x.dtype)
```

```python
import functools

import numpy as np
import jax
import jax.numpy as jnp
from jax import lax
from jax.experimental import pallas as pl
from jax.experimental.pallas import tpu as pltpu

F32 = jnp.float32
BF16 = jnp.bfloat16

EPS = 1e-6
NEG_BIG = -1e30
LOG2_E = 1.4426950408889634

GRID_W = 64
GLA_HEADS, GLA_DK, GLA_DV = 4, 128, 256
GLA_RANK = 16
GLA_TAU = 16.0
GLA_CHUNK = 64
GLA_LEVELS = (8, 16, 32, 64)
GLA_PHASES = 4
NA_HEADS, NA_HD = 8, 64
NA_ROWS, NA_COLS = 8, 16
NA_GROUP = 4
MEM_HEADS, MEM_HD = 4, 128

COL_TILE = 512
N_MAIN = 9216
T_GQ, T_GK, T_GV, T_GGATE = 0, 1, 2, 4
T_NQ, T_NK, T_NV, T_NGATE = 6, 7, 8, 9
T_MQ, T_MGATE, T_MERGE = 10, 11, 12
N_COL_TILES = N_MAIN // COL_TILE
IN_PROJ_SUB_ROWS = 256

VMEM_LIMIT_BYTES = 48 * 1024 * 1024


def _dot(a, b):
    return jnp.dot(a, b, preferred_element_type=F32)


def _dot_nt(a, b):
    return lax.dot_general(a, b, (((1,), (1,)), ((), ())), preferred_element_type=F32)


def _dot_tn(a, b):
    return lax.dot_general(a, b, (((0,), (0,)), ((), ())), preferred_element_type=F32)


def _in_proj_kernel(x_ref, g_ref, w_ref, wlr_ref, gain_ref, proj_ref, glr_ref, h_ref):
    j = pl.program_id(1)

    @pl.when(j == 0)
    def _():
        xf = x_ref[...]
        ms = jnp.mean(xf * xf, axis=-1, keepdims=True)
        h = (xf * lax.rsqrt(ms + EPS) * g_ref[...]).astype(BF16)
        h_ref[...] = h
        glr_ref[...] = _dot(h, wlr_ref[...])

    gain = gain_ref[0]

    def tile(epilogue):
        for r0 in range(0, h_ref.shape[0], IN_PROJ_SUB_ROWS):
            rows = slice(r0, r0 + IN_PROJ_SUB_ROWS)
            acc = _dot(h_ref[rows, :], w_ref[...])
            proj_ref[rows, :] = epilogue(acc).astype(BF16)

    def normed(head_w):
        def epilogue(acc):
            sq = acc * acc
            lane = lax.broadcasted_iota(jnp.int32, (acc.shape[0], 128), 1)
            groups = []
            for c0 in range(0, COL_TILE, 128):
                sq_g = sq[:, c0:c0 + 128]
                scale = None
                for h0 in range(0, 128, head_w):
                    mine = (lane >= h0) & (lane < h0 + head_w)
                    ms = jnp.sum(jnp.where(mine, sq_g, 0.0), axis=-1, keepdims=True) * (1.0 / head_w)
                    r = lax.rsqrt(ms + EPS)
                    scale = r if scale is None else jnp.where(mine, r, scale)
                groups.append(jnp.broadcast_to(scale, sq_g.shape))
            return acc * jnp.concatenate(groups, axis=1) * gain
        return epilogue

    is_silu = (j == T_GGATE) | (j == T_GGATE + 1) | (j == T_NGATE) | (j == T_MGATE)
    is_sig = j >= T_MERGE
    is_n64 = (j == T_NQ) | (j == T_NK)
    is_n128 = j == T_MQ
    is_lin = jnp.logical_not(is_silu | is_sig | is_n64 | is_n128)

    pl.when(is_lin)(lambda: tile(lambda acc: acc * gain))
    pl.when(is_silu)(lambda: tile(lambda acc: acc * jax.nn.sigmoid(acc)))
    pl.when(is_sig)(lambda: tile(jax.nn.sigmoid))
    pl.when(is_n64)(lambda: tile(normed(NA_HD)))
    pl.when(is_n128)(lambda: tile(normed(MEM_HD)))


def _in_proj(x2, g, w_main, w_lr, gains, tm):
    m, d = x2.shape
    return pl.pallas_call(
        _in_proj_kernel,
        out_shape=(jax.ShapeDtypeStruct((m, N_MAIN), BF16),
                   jax.ShapeDtypeStruct((m, 128), F32)),
        grid=(m // tm, N_COL_TILES),
        in_specs=[
            pl.BlockSpec((tm, d), lambda i, j: (i, 0)),
            pl.BlockSpec((1, d), lambda i, j: (0, 0)),
            pl.BlockSpec((d, COL_TILE), lambda i, j: (0, j)),
            pl.BlockSpec((d, 128), lambda i, j: (0, 0)),
            pl.BlockSpec((1, 1, COL_TILE), lambda i, j: (j, 0, 0)),
        ],
        out_specs=(pl.BlockSpec((tm, COL_TILE), lambda i, j: (i, j)),
                   pl.BlockSpec((tm, 128), lambda i, j: (i, 0))),
        scratch_shapes=[pltpu.VMEM((tm, d), BF16)],
        compiler_params=pltpu.CompilerParams(
            dimension_semantics=("arbitrary", "arbitrary"),
            vmem_limit_bytes=VMEM_LIMIT_BYTES),
        name="in_proj",
    )(x2, g, w_main, w_lr, gains)


def _run_phase_major(chains, n_phases):
    for _ in range(n_phases):
        for chain in chains:
            next(chain)


def _gla_constants():
    c = GLA_CHUNK
    idx = np.arange(c)
    tri = np.zeros((2, 2 * len(GLA_LEVELS) * c, c), np.float32)
    msk = np.zeros((2, len(GLA_LEVELS), c, c), np.float32)
    for d in range(2):
        covered = np.zeros((c, c), bool)
        for n, l in enumerate(GLA_LEVELS):
            blk = idx // l
            same = blk[:, None] == blk[None, :]
            if d == 0:
                wq = same & (idx[None, :] <= idx[:, None])
                wk = same & (idx[None, :] > idx[:, None])
                adj = blk[:, None] == blk[None, :] + 1
            else:
                wq = same & (idx[None, :] >= idx[:, None])
                wk = same & (idx[None, :] < idx[:, None])
                adj = blk[None, :] == blk[:, None] + 1
            tri[d, n * c:(n + 1) * c] = wq
            tri[d, (len(GLA_LEVELS) + n) * c:(len(GLA_LEVELS) + n + 1) * c] = wk
            if l != c:
                m = adj & ~covered
                covered |= m
                msk[d, n] = m
    tri = np.concatenate([tri, tri], axis=2)
    return jnp.asarray(tri, BF16), jnp.asarray(msk[:, :len(GLA_LEVELS) - 1], F32)


def _log_sigmoid(z):
    return jnp.minimum(z, 0.0) - jnp.log(1.0 + jnp.exp(-jnp.abs(z)))


def _gla_chunk(d, head, r0, q_ref, k_ref, v_ref, g_scr, st_scr, o_ref, tri_ref, msk_ref):
    c, nl = GLA_CHUNK, len(GLA_LEVELS)
    ks = slice(head * GLA_DK, (head + 1) * GLA_DK)
    vs = slice(head * GLA_DV, (head + 1) * GLA_DV)
    rows = pl.ds(r0, c)
    q = q_ref[0, rows, ks].astype(F32)
    k = k_ref[0, rows, ks].astype(F32)
    v = v_ref[0, rows, vs]
    g = g_scr[rows, ks]

    g_hi = g.astype(BF16)
    g_lo = (g - g_hi.astype(F32)).astype(BF16)
    sums = _dot(tri_ref[d], jnp.concatenate([g_hi, g_lo], axis=0))
    yield
    e = jnp.exp2(sums)

    def lvl(n):
        eq = e[n * c:(n + 1) * c]
        ek = e[(nl + n) * c:(nl + n + 1) * c]
        return (q * eq).astype(BF16), (k * ek).astype(BF16)

    lvl_dots = []
    for n in range(nl - 1):
        qn, kn = lvl(n)
        lvl_dots.append(_dot_nt(qn, kn))
    q64, k64 = lvl(nl - 1)
    st = st_scr[head]
    o_inter = _dot_nt(q64, st.astype(BF16))
    st_update = _dot_tn(v, k64)
    yield

    w8 = sums[0:c]
    row = lax.broadcasted_iota(jnp.int32, (8, GLA_DK), 0)
    lane = lax.broadcasted_iota(jnp.int32, (8, c), 1)
    diag_blocks = []
    for p in range(c // 8):
        wp, qp, kp = w8[8 * p:8 * p + 8], q[8 * p:8 * p + 8], k[8 * p:8 * p + 8]
        ad = jnp.zeros((8, c), F32)
        for jj in range(8):
            valid = (row >= jj) if d == 0 else (row <= jj)
            ex = jnp.exp2(jnp.where(valid, wp - wp[jj:jj + 1], NEG_BIG))
            col = jnp.sum(qp * ex * kp[jj:jj + 1], axis=-1, keepdims=True)
            ad = jnp.where(lane == 8 * p + jj, col, ad)
        diag_blocks.append(ad)
    a = jnp.concatenate(diag_blocks, axis=0)
    yield

    for n in range(nl - 1):
        a = a + msk_ref[d, n] * lvl_dots[n]
    o = _dot(a.astype(BF16), v) + o_inter
    o_ref[0, rows, vs] = o.astype(o_ref.dtype)

    tot_row = (nl - 1) * c + (c - 1 if d == 0 else 0)
    st_scr[head] = st * e[tot_row:tot_row + 1] + st_update
    yield


def _gla_kernel(qf_ref, kf_ref, vf_ref, lrf_ref, qb_ref, kb_ref, vb_ref, lrb_ref,
                w2f_ref, bf_ref, w2b_ref, bb_ref, tri_ref, msk_ref,
                of_ref, ob_ref, stf_scr, stb_scr, gf_scr, gb_scr):
    t = pl.program_id(1)

    @pl.when(t == 0)
    def _():
        stf_scr[...] = jnp.zeros_like(stf_scr)
        stb_scr[...] = jnp.zeros_like(stb_scr)

    to_log2_decay = LOG2_E / GLA_TAU
    zf = _dot(lrf_ref[0].astype(BF16), w2f_ref[...]) + bf_ref[...]
    gf_scr[...] = _log_sigmoid(zf) * to_log2_decay
    zb = _dot(lrb_ref[0].astype(BF16), w2b_ref[...]) + bb_ref[...]
    gb_scr[...] = _log_sigmoid(zb) * to_log2_decay

    n_chunks = qf_ref.shape[1] // GLA_CHUNK

    def body(ci, carry):
        rf = pl.multiple_of(ci * GLA_CHUNK, GLA_CHUNK)
        rb = pl.multiple_of((n_chunks - 1 - ci) * GLA_CHUNK, GLA_CHUNK)
        streams = []
        for head in range(GLA_HEADS):
            streams.append(_gla_chunk(0, head, rf, qf_ref, kf_ref, vf_ref, gf_scr, stf_scr,
                                      of_ref, tri_ref, msk_ref))
            streams.append(_gla_chunk(1, head, rb, qb_ref, kb_ref, vb_ref, gb_scr, stb_scr,
                                      ob_ref, tri_ref, msk_ref))
        _run_phase_major(streams, GLA_PHASES)
        return carry

    lax.fori_loop(0, n_chunks, body, 0)


def _gla(proj3, glr3, w2f, bf, w2b, bb, tri, msk, tb):
    b, s, _ = proj3.shape
    nt = s // tb
    qk_w, v_w = GLA_HEADS * GLA_DK, GLA_HEADS * GLA_DV
    fwd = lambda col: (lambda bi, t: (bi, t, col))
    bwd = lambda col: (lambda bi, t: (bi, nt - 1 - t, col))
    const2 = lambda bi, t: (0, 0)
    in_specs = []
    for mk in (fwd, bwd):
        in_specs += [
            pl.BlockSpec((1, tb, qk_w), mk(0)),
            pl.BlockSpec((1, tb, qk_w), mk(1)),
            pl.BlockSpec((1, tb, v_w), mk(1)),
            pl.BlockSpec((1, tb, 128), mk(0)),
        ]
    in_specs += [
        pl.BlockSpec((128, qk_w), const2), pl.BlockSpec((1, qk_w), const2),
        pl.BlockSpec((128, qk_w), const2), pl.BlockSpec((1, qk_w), const2),
        pl.BlockSpec(tri.shape, lambda bi, t: (0, 0, 0)),
        pl.BlockSpec(msk.shape, lambda bi, t: (0, 0, 0, 0)),
    ]
    return pl.pallas_call(
        _gla_kernel,
        out_shape=(jax.ShapeDtypeStruct((b, s, v_w), BF16),
                   jax.ShapeDtypeStruct((b, s, v_w), BF16)),
        grid=(b, nt),
        in_specs=in_specs,
        out_specs=(pl.BlockSpec((1, tb, v_w), fwd(0)),
                   pl.BlockSpec((1, tb, v_w), bwd(0))),
        scratch_shapes=[pltpu.VMEM((GLA_HEADS, GLA_DV, GLA_DK), F32),
                        pltpu.VMEM((GLA_HEADS, GLA_DV, GLA_DK), F32),
                        pltpu.VMEM((tb, qk_w), F32),
                        pltpu.VMEM((tb, qk_w), F32)],
        compiler_params=pltpu.CompilerParams(
            dimension_semantics=("arbitrary", "arbitrary"),
            vmem_limit_bytes=VMEM_LIMIT_BYTES),
        name="gla",
    )(proj3, proj3, proj3, glr3, proj3, proj3, proj3, glr3, w2f, bf, w2b, bb, tri, msk)


def _na_bias_table(rpb):
    col = np.arange(GRID_W)
    cs = np.clip(col - NA_COLS // 2, 0, GRID_W - NA_COLS)
    inside = (col[None, :] >= cs[:, None]) & (col[None, :] < cs[:, None] + NA_COLS)
    dc = col[None, :] - col[:, None] + (NA_COLS - 1)
    onehot = (np.arange(2 * NA_COLS - 1)[:, None, None] == dc[None]) & inside[None]
    t = jnp.einsum("hdc,cqk->hdqk", rpb.astype(F32), jnp.asarray(onehot, F32),
                   precision=lax.Precision.HIGHEST)
    t = jnp.where(inside[None, None], t, NEG_BIG)
    return jnp.concatenate([t[:, :-1], t[:, 1:]], axis=-1)


def _na_kernel(q_ref, k_ref, v_ref, t2_ref, o_ref, *, n_rows):
    rb = q_ref.shape[1] // GRID_W
    t = pl.program_id(1)
    gw = NA_GROUP * NA_HD
    head_of_lane = lax.broadcasted_iota(jnp.int32, (GRID_W, gw), 1) // NA_HD
    n_keys = NA_ROWS * GRID_W

    def pick(blocks):
        out = blocks[(NA_GROUP - 1) * GRID_W:]
        for h in range(NA_GROUP - 2, -1, -1):
            out = jnp.where(head_of_lane == h, blocks[h * GRID_W:(h + 1) * GRID_W], out)
        return out

    def body(rr, carry):
        r = t * rb + rr
        rs = jnp.clip(r - NA_ROWS // 2, 0, n_rows - NA_ROWS)
        off = rs - r + (NA_ROWS - 1)
        qrows = pl.ds(pl.multiple_of(rr * GRID_W, GRID_W), GRID_W)
        krows = pl.ds(pl.multiple_of(rs * GRID_W, GRID_W), n_keys)
        for grp in range(NA_HEADS // NA_GROUP):
            ls = slice(grp * gw, (grp + 1) * gw)
            q4 = q_ref[0, qrows, ls]
            k4 = k_ref[0, krows, ls]
            v4 = v_ref[0, krows, ls]
            lhs = jnp.concatenate(
                [jnp.where(head_of_lane == h, q4, jnp.zeros_like(q4)) for h in range(NA_GROUP)],
                axis=0)
            bias = jnp.concatenate(
                [jnp.concatenate([t2_ref[grp * NA_GROUP + h, off + 2 * w]
                                  for w in range(NA_ROWS // 2)], axis=1)
                 for h in range(NA_GROUP)], axis=0)
            s = _dot_nt(lhs, k4) + bias
            m = jnp.max(s, axis=-1, keepdims=True)
            pe = jnp.exp(s - m)
            den = jnp.sum(pe, axis=-1, keepdims=True)
            pv = _dot(pe.astype(BF16), v4)
            o_ref[0, qrows, ls] = (pick(pv) / pick(den)).astype(o_ref.dtype)
        return carry

    lax.fori_loop(0, rb, body, 0, unroll=True)


def _na(proj3, t2, rb):
    b, s, _ = proj3.shape
    n_rows = s // GRID_W
    w = NA_HEADS * NA_HD
    return pl.pallas_call(
        functools.partial(_na_kernel, n_rows=n_rows),
        out_shape=jax.ShapeDtypeStruct((b, s, w), BF16),
        grid=(b, n_rows // rb),
        in_specs=[
            pl.BlockSpec((1, rb * GRID_W, w), lambda bi, t: (bi, t, T_NQ)),
            pl.BlockSpec((1, s, w), lambda bi, t: (bi, 0, T_NK)),
            pl.BlockSpec((1, s, w), lambda bi, t: (bi, 0, T_NV)),
            pl.BlockSpec(t2.shape, lambda bi, t: (0, 0, 0, 0)),
        ],
        out_specs=pl.BlockSpec((1, rb * GRID_W, w), lambda bi, t: (bi, t, 0)),
        compiler_params=pltpu.CompilerParams(
            dimension_semantics=("arbitrary", "arbitrary"),
            vmem_limit_bytes=VMEM_LIMIT_BYTES),
        name="na",
    )(proj3, proj3, proj3, t2)


def _mem_kv_kernel(mem_ref, g_ref, w_ref, kg_ref, k_ref, v_ref):
    xf = mem_ref[0]
    ms = jnp.mean(xf * xf, axis=-1, keepdims=True)
    h = (xf * lax.rsqrt(ms + EPS) * g_ref[...]).astype(BF16)
    kv = _dot(h, w_ref[...])
    wk = MEM_HEADS * MEM_HD
    parts = []
    for hd in range(MEM_HEADS):
        kh = kv[:, hd * MEM_HD:(hd + 1) * MEM_HD]
        msk = jnp.mean(kh * kh, axis=-1, keepdims=True)
        parts.append(kh * lax.rsqrt(msk + EPS) * kg_ref[...])
    k_ref[0] = jnp.concatenate(parts, axis=1).astype(BF16)
    v_ref[0] = kv[:, wk:].astype(BF16)


def _mem_kv(mem, g, w, kg):
    b, m, d = mem.shape
    wk = MEM_HEADS * MEM_HD
    return pl.pallas_call(
        _mem_kv_kernel,
        out_shape=(jax.ShapeDtypeStruct((b, m, wk), BF16),
                   jax.ShapeDtypeStruct((b, m, wk), BF16)),
        grid=(b,),
        in_specs=[
            pl.BlockSpec((1, m, d), lambda bi: (bi, 0, 0)),
            pl.BlockSpec((1, d), lambda bi: (0, 0)),
            pl.BlockSpec((d, 2 * wk), lambda bi: (0, 0)),
            pl.BlockSpec((1, MEM_HD), lambda bi: (0, 0)),
        ],
        out_specs=(pl.BlockSpec((1, m, wk), lambda bi: (bi, 0, 0)),
                   pl.BlockSpec((1, m, wk), lambda bi: (bi, 0, 0))),
        compiler_params=pltpu.CompilerParams(
            dimension_semantics=("arbitrary",),
            vmem_limit_bytes=VMEM_LIMIT_BYTES),
        name="mem_kv",
    )(mem, g, w, kg)


def _merge_kernel(of_ref, ob_ref, ggate_ref, nb_ref, ngate_ref, mq_ref, mgate_ref, sig_ref,
                  x_ref, kc_ref, vc_ref, gout_ref, pa_ref, pb_ref, pc_ref, wo_ref, out_ref):
    d = x_ref.shape[1]
    o = of_ref[...].astype(F32) + ob_ref[...].astype(F32)
    parts = []
    for h in range(GLA_HEADS):
        oh = o[:, h * GLA_DV:(h + 1) * GLA_DV]
        ms = jnp.mean(oh * oh, axis=-1, keepdims=True)
        parts.append(oh * lax.rsqrt(ms + EPS) * gout_ref[...])
    oa = jnp.concatenate(parts, axis=1) * ggate_ref[...].astype(F32)
    ya = _dot(oa.astype(BF16), pa_ref[...])

    nb = nb_ref[...].astype(F32) * ngate_ref[...].astype(F32)
    yb = _dot(nb.astype(BF16), pb_ref[...])

    parts = []
    for h in range(MEM_HEADS):
        hs = slice(h * MEM_HD, (h + 1) * MEM_HD)
        s = _dot_nt(mq_ref[:, hs], kc_ref[0, :, hs])
        m = jnp.max(s, axis=-1, keepdims=True)
        pe = jnp.exp(s - m)
        den = jnp.sum(pe, axis=-1, keepdims=True)
        parts.append(_dot(pe.astype(BF16), vc_ref[0, :, hs]) / den)
    oc = jnp.concatenate(parts, axis=1) * mgate_ref[...].astype(F32)
    yc = _dot(oc.astype(BF16), pc_ref[...])

    y = (sig_ref[:, 0:d].astype(F32) * ya + sig_ref[:, d:2 * d].astype(F32) * yb
         + sig_ref[:, 2 * d:3 * d].astype(F32) * yc)
    out_ref[...] = x_ref[...] + _dot(y.astype(BF16), wo_ref[...])


def _merge(of2, ob2, nb2, proj2, x2, kc, vc, gout, pa, pb, pc, wo, tm, seq):
    m, d = x2.shape
    v_w = GLA_HEADS * GLA_DV
    na_w = NA_HEADS * NA_HD
    mem_w = MEM_HEADS * MEM_HD
    steps_per_batch = seq // tm
    row = lambda col: (lambda i: (i, col))
    const = lambda i: (0, 0)
    batch = lambda i: (i // steps_per_batch, 0, 0)
    return pl.pallas_call(
        _merge_kernel,
        out_shape=jax.ShapeDtypeStruct((m, d), F32),
        grid=(m // tm,),
        in_specs=[
            pl.BlockSpec((tm, v_w), row(0)),
            pl.BlockSpec((tm, v_w), row(0)),
            pl.BlockSpec((tm, v_w), row(T_GGATE * COL_TILE // v_w)),
            pl.BlockSpec((tm, na_w), row(0)),
            pl.BlockSpec((tm, na_w), row(T_NGATE)),
            pl.BlockSpec((tm, mem_w), row(T_MQ)),
            pl.BlockSpec((tm, mem_w), row(T_MGATE)),
            pl.BlockSpec((tm, 3 * d), row(T_MERGE * COL_TILE // (3 * d))),
            pl.BlockSpec((tm, d), row(0)),
            pl.BlockSpec((1,) + kc.shape[1:], batch),
            pl.BlockSpec((1,) + vc.shape[1:], batch),
            pl.BlockSpec((1, GLA_DV), const),
            pl.BlockSpec(pa.shape, const),
            pl.BlockSpec(pb.shape, const),
            pl.BlockSpec(pc.shape, const),
            pl.BlockSpec(wo.shape, const),
        ],
        out_specs=pl.BlockSpec((tm, d), row(0)),
        compiler_params=pltpu.CompilerParams(
            dimension_semantics=("arbitrary",),
            vmem_limit_bytes=VMEM_LIMIT_BYTES),
        name="merge",
    )(of2, ob2, proj2, nb2, proj2, proj2, proj2, proj2, x2, kc, vc, gout, pa, pb, pc, wo)


def _layer_params(l, norm_g, w_in, gla_w2_f, gla_b_f, gla_w2_b, gla_b_b, gla_out_g, p_a,
                  na_q_g, na_k_g, na_rpb, p_b, mem_norm_g, w_mem_kv, mem_q_g, mem_k_g, p_c,
                  w_out):
    d = w_in.shape[1]
    qk_w = GLA_HEADS * GLA_DK
    lr0 = 2 * qk_w + 2 * GLA_HEADS * GLA_DV
    lr1 = lr0 + 2 * GLA_RANK
    w = w_in[l]
    w_main = jnp.concatenate([w[:, :lr0], w[:, lr1:]], axis=1).astype(BF16)
    w_lr = jnp.pad(w[:, lr0:lr1], ((0, 0), (0, 128 - 2 * GLA_RANK))).astype(BF16)

    gains = jnp.ones((N_COL_TILES, COL_TILE), F32)
    gains = gains.at[T_GQ].set(GLA_DK ** -0.5)
    gains = gains.at[T_NQ].set(jnp.tile(na_q_g[l].astype(F32) * NA_HD ** -0.5, NA_HEADS))
    gains = gains.at[T_NK].set(jnp.tile(na_k_g[l].astype(F32), NA_HEADS))
    gains = gains.at[T_MQ].set(jnp.tile(mem_q_g[l].astype(F32) * MEM_HD ** -0.5, MEM_HEADS))

    w2f = jnp.zeros((128, qk_w), F32).at[:GLA_RANK].set(gla_w2_f[l]).astype(BF16)
    w2b = jnp.zeros((128, qk_w), F32).at[GLA_RANK:2 * GLA_RANK].set(gla_w2_b[l]).astype(BF16)
    return dict(
        norm_g=norm_g[l].reshape(1, d).astype(F32),
        w_main=w_main, w_lr=w_lr, gains=gains.reshape(N_COL_TILES, 1, COL_TILE),
        w2f=w2f, bf=gla_b_f[l].reshape(1, qk_w).astype(F32),
        w2b=w2b, bb=gla_b_b[l].reshape(1, qk_w).astype(F32),
        gout=gla_out_g[l].reshape(1, GLA_DV).astype(F32),
        t2=_na_bias_table(na_rpb[l]),
        mem_g=mem_norm_g[l].reshape(1, d).astype(F32),
        w_kv=w_mem_kv[l].astype(BF16),
        mem_kg=mem_k_g[l].reshape(1, MEM_HD).astype(F32),
        pa=p_a[l].astype(BF16), pb=p_b[l].astype(BF16), pc=p_c[l].astype(BF16),
        wo=w_out[l].astype(BF16),
    )


def _tiles(batch, seq):
    m = batch * seq
    tm_in = 2048 if m % 2048 == 0 else seq
    tm_merge = 512 if seq % 512 == 0 else seq
    tb_gla = 512 if seq % 512 == 0 else seq
    rb_na = 8
    return tm_in, tm_merge, tb_gla, rb_na


def kernel(x, mem, norm_g, w_in, gla_w2_f, gla_b_f, gla_w2_b, gla_b_b, gla_out_g, p_a, na_q_g,
           na_k_g, na_rpb, p_b, mem_norm_g, w_mem_kv, mem_q_g, mem_k_g, p_c, w_out):
    b, s, d = x.shape
    depth = w_in.shape[0]
    n_rows = s // GRID_W
    assert s % GRID_W == 0 and n_rows >= NA_ROWS and s % GLA_CHUNK == 0
    assert w_in.shape[2] - 2 * GLA_RANK == N_MAIN
    tm_in, tm_merge, tb_gla, rb_na = _tiles(b, s)
    tri, msk = _gla_constants()

    x2 = x.reshape(b * s, d).astype(F32)
    for l in range(depth):
        p = _layer_params(l, norm_g, w_in, gla_w2_f, gla_b_f, gla_w2_b, gla_b_b, gla_out_g,
                          p_a, na_q_g, na_k_g, na_rpb, p_b, mem_norm_g, w_mem_kv, mem_q_g,
                          mem_k_g, p_c, w_out)
        proj2, glr2 = _in_proj(x2, p["norm_g"], p["w_main"], p["w_lr"], p["gains"], tm_in)
        proj3 = proj2.reshape(b, s, N_MAIN)
        glr3 = glr2.reshape(b, s, 128)
        o_f, o_b = _gla(proj3, glr3, p["w2f"], p["bf"], p["w2b"], p["bb"], tri, msk, tb_gla)
        nb = _na(proj3, p["t2"], rb_na)
        kc, vc = _mem_kv(mem.astype(F32), p["mem_g"], p["w_kv"], p["mem_kg"])
        x2 = _merge(o_f.reshape(b * s, -1), o_b.reshape(b * s, -1), nb.reshape(b * s, -1),
                    proj2, x2, kc, vc, p["gout"], p["pa"], p["pb"], p["pc"], p["wo"],
                    tm_merge, s)
    return x2.reshape(b, s, d).astype(x.dtype)
```

```python
import functools

import numpy as np
import jax
import jax.numpy as jnp
from jax import lax
from jax.experimental import pallas as pl
from jax.experimental.pallas import tpu as pltpu

F32 = jnp.float32
BF16 = jnp.bfloat16

EPS = 1e-6
NEG_BIG = -1e30
LOG2_E = 1.4426950408889634

GRID_W = 64
GLA_HEADS, GLA_DK, GLA_DV = 4, 128, 256
GLA_RANK = 16
GLA_TAU = 16.0
GLA_CHUNK = 64
GLA_LEVELS = (8, 16, 32, 64)
GLA_PHASES = 4
GLA_CHUNK_UNROLL = 4
NA_HEADS, NA_HD = 8, 64
NA_ROWS, NA_COLS = 8, 16
NA_GROUP = 4
MEM_HEADS, MEM_HD = 4, 128

COL_TILE = 512
N_MAIN = 9216
T_GQ, T_GK, T_GV, T_GGATE = 0, 1, 2, 4
T_NQ, T_NK, T_NV, T_NGATE = 6, 7, 8, 9
T_MQ, T_MGATE, T_MERGE = 10, 11, 12
N_COL_TILES = N_MAIN // COL_TILE
IN_PROJ_SUB_ROWS = 128

VMEM_LIMIT_BYTES = 48 * 1024 * 1024


def _dot(a, b):
    return jnp.dot(a, b, preferred_element_type=F32)


def _dot_nt(a, b):
    return lax.dot_general(a, b, (((1,), (1,)), ((), ())), preferred_element_type=F32)


def _dot_tn(a, b):
    return lax.dot_general(a, b, (((0,), (0,)), ((), ())), preferred_element_type=F32)


def _in_proj_kernel(x_ref, g_ref, w_ref, wlr_ref, gain_ref, proj_ref, glr_ref, h_ref):
    j = pl.program_id(1)

    @pl.when(j == 0)
    def _():
        xf = x_ref[...]
        ms = jnp.mean(xf * xf, axis=-1, keepdims=True)
        h = (xf * lax.rsqrt(ms + EPS) * g_ref[...]).astype(BF16)
        h_ref[...] = h
        glr_ref[...] = _dot(h, wlr_ref[...])

    gain = gain_ref[0]

    def tile(epilogue):
        for r0 in range(0, h_ref.shape[0], IN_PROJ_SUB_ROWS):
            rows = slice(r0, r0 + IN_PROJ_SUB_ROWS)
            acc = _dot(h_ref[rows, :], w_ref[...])
            proj_ref[rows, :] = epilogue(acc).astype(BF16)

    def normed(head_w):
        def epilogue(acc):
            sq = acc * acc
            lane = lax.broadcasted_iota(jnp.int32, (acc.shape[0], 128), 1)
            groups = []
            for c0 in range(0, COL_TILE, 128):
                sq_g = sq[:, c0:c0 + 128]
                scale = None
                for h0 in range(0, 128, head_w):
                    mine = (lane >= h0) & (lane < h0 + head_w)
                    ms = jnp.sum(jnp.where(mine, sq_g, 0.0), axis=-1, keepdims=True) * (1.0 / head_w)
                    r = lax.rsqrt(ms + EPS)
                    scale = r if scale is None else jnp.where(mine, r, scale)
                groups.append(jnp.broadcast_to(scale, sq_g.shape))
            return acc * jnp.concatenate(groups, axis=1) * gain
        return epilogue

    is_silu = (j == T_GGATE) | (j == T_GGATE + 1) | (j == T_NGATE) | (j == T_MGATE)
    is_sig = j >= T_MERGE
    is_n64 = (j == T_NQ) | (j == T_NK)
    is_n128 = j == T_MQ
    is_lin = jnp.logical_not(is_silu | is_sig | is_n64 | is_n128)

    pl.when(is_lin)(lambda: tile(lambda acc: acc * gain))
    pl.when(is_silu)(lambda: tile(lambda acc: acc * jax.nn.sigmoid(acc)))
    pl.when(is_sig)(lambda: tile(jax.nn.sigmoid))
    pl.when(is_n64)(lambda: tile(normed(NA_HD)))
    pl.when(is_n128)(lambda: tile(normed(MEM_HD)))


def _in_proj(x2, g, w_main, w_lr, gains, tm):
    m, d = x2.shape
    return pl.pallas_call(
        _in_proj_kernel,
        out_shape=(jax.ShapeDtypeStruct((m, N_MAIN), BF16),
                   jax.ShapeDtypeStruct((m, 128), F32)),
        grid=(m // tm, N_COL_TILES),
        in_specs=[
            pl.BlockSpec((tm, d), lambda i, j: (i, 0)),
            pl.BlockSpec((1, d), lambda i, j: (0, 0)),
            pl.BlockSpec((d, COL_TILE), lambda i, j: (0, j)),
            pl.BlockSpec((d, 128), lambda i, j: (0, 0)),
            pl.BlockSpec((1, 1, COL_TILE), lambda i, j: (j, 0, 0)),
        ],
        out_specs=(pl.BlockSpec((tm, COL_TILE), lambda i, j: (i, j)),
                   pl.BlockSpec((tm, 128), lambda i, j: (i, 0))),
        scratch_shapes=[pltpu.VMEM((tm, d), BF16)],
        compiler_params=pltpu.CompilerParams(
            dimension_semantics=("arbitrary", "arbitrary"),
            vmem_limit_bytes=VMEM_LIMIT_BYTES),
        name="in_proj",
    )(x2, g, w_main, w_lr, gains)


def _run_phase_major(chains, n_phases):
    for _ in range(n_phases):
        for chain in chains:
            next(chain)


def _gla_constants():
    c = GLA_CHUNK
    idx = np.arange(c)
    tri = np.zeros((2, 2 * len(GLA_LEVELS) * c, c), np.float32)
    msk = np.zeros((2, len(GLA_LEVELS), c, c), np.float32)
    for d in range(2):
        covered = np.zeros((c, c), bool)
        for n, l in enumerate(GLA_LEVELS):
            blk = idx // l
            same = blk[:, None] == blk[None, :]
            if d == 0:
                wq = same & (idx[None, :] <= idx[:, None])
                wk = same & (idx[None, :] > idx[:, None])
                adj = blk[:, None] == blk[None, :] + 1
            else:
                wq = same & (idx[None, :] >= idx[:, None])
                wk = same & (idx[None, :] < idx[:, None])
                adj = blk[None, :] == blk[:, None] + 1
            tri[d, n * c:(n + 1) * c] = wq
            tri[d, (len(GLA_LEVELS) + n) * c:(len(GLA_LEVELS) + n + 1) * c] = wk
            if l != c:
                m = adj & ~covered
                covered |= m
                msk[d, n] = m
    tri = np.concatenate([tri, tri], axis=2)
    return jnp.asarray(tri, BF16), jnp.asarray(msk[:, :len(GLA_LEVELS) - 1], F32)


def _log_sigmoid(z):
    return jnp.minimum(z, 0.0) - jnp.log(1.0 + jnp.exp(-jnp.abs(z)))


def _gla_chunk(d, head, r0, q_ref, k_ref, v_ref, g_scr, st_scr, o_ref, tri_ref, msk_ref,
               diag_masks):
    c, nl = GLA_CHUNK, len(GLA_LEVELS)
    ks = slice(head * GLA_DK, (head + 1) * GLA_DK)
    vs = slice(head * GLA_DV, (head + 1) * GLA_DV)
    rows = pl.ds(r0, c)
    q = q_ref[0, rows, ks].astype(F32)
    k = k_ref[0, rows, ks].astype(F32)
    v = v_ref[0, rows, vs]
    g = g_scr[rows, ks]

    g_hi = g.astype(BF16)
    g_lo = (g - g_hi.astype(F32)).astype(BF16)
    sums = _dot(tri_ref[d], jnp.concatenate([g_hi, g_lo], axis=0))
    yield
    e = jnp.exp2(sums)

    def lvl(n):
        eq = e[n * c:(n + 1) * c]
        ek = e[(nl + n) * c:(nl + n + 1) * c]
        return (q * eq).astype(BF16), (k * ek).astype(BF16)

    lvl_dots = []
    for n in range(nl - 1):
        qn, kn = lvl(n)
        lvl_dots.append(_dot_nt(qn, kn))
    q64, k64 = lvl(nl - 1)
    st = st_scr[head]
    o_inter = _dot_nt(q64, st.astype(BF16))
    st_update = _dot_tn(v, k64)
    yield

    blocks = lambda x: x.reshape(c // 8, 8, GLA_DK)
    step = 1 if d == 0 else 7
    q3, kd, decay = blocks(q), blocks(k), blocks(jnp.exp2(g))
    a = jnp.where(diag_masks[d][0], jnp.sum(q * k, axis=-1, keepdims=True), 0.0)
    for lag in range(1, 8):
        kd = decay * pltpu.roll(kd, step, 1)
        col = jnp.sum(q3 * kd, axis=-1, keepdims=True).reshape(c, 1)
        a = jnp.where(diag_masks[d][lag], col, a)
    yield

    for n in range(nl - 1):
        a = a + msk_ref[d, n] * lvl_dots[n]
    o = _dot(a.astype(BF16), v) + o_inter
    o_ref[0, rows, vs] = o.astype(o_ref.dtype)

    tot_row = (nl - 1) * c + (c - 1 if d == 0 else 0)
    st_scr[head] = st * e[tot_row:tot_row + 1] + st_update
    yield


def _gla_kernel(qf_ref, kf_ref, vf_ref, lrf_ref, qb_ref, kb_ref, vb_ref, lrb_ref,
                w2f_ref, bf_ref, w2b_ref, bb_ref, tri_ref, msk_ref,
                of_ref, ob_ref, stf_scr, stb_scr, gf_scr, gb_scr):
    t = pl.program_id(1)

    @pl.when(t == 0)
    def _():
        stf_scr[...] = jnp.zeros_like(stf_scr)
        stb_scr[...] = jnp.zeros_like(stb_scr)

    to_log2_decay = LOG2_E / GLA_TAU
    zf = _dot(lrf_ref[0].astype(BF16), w2f_ref[...]) + bf_ref[...]
    gf_scr[...] = _log_sigmoid(zf) * to_log2_decay
    zb = _dot(lrb_ref[0].astype(BF16), w2b_ref[...]) + bb_ref[...]
    gb_scr[...] = _log_sigmoid(zb) * to_log2_decay

    n_chunks = qf_ref.shape[1] // GLA_CHUNK

    row = lax.broadcasted_iota(jnp.int32, (GLA_CHUNK, GLA_CHUNK), 0)
    col = lax.broadcasted_iota(jnp.int32, (GLA_CHUNK, GLA_CHUNK), 1)
    same_block = (row // 8) == (col // 8)
    diag_masks = [[same_block & (col == row - lag) for lag in range(8)],
                  [same_block & (col == row + lag) for lag in range(8)]]

    def body(ci, carry):
        rf = pl.multiple_of(ci * GLA_CHUNK, GLA_CHUNK)
        rb = pl.multiple_of((n_chunks - 1 - ci) * GLA_CHUNK, GLA_CHUNK)
        streams = []
        for head in range(GLA_HEADS):
            streams.append(_gla_chunk(0, head, rf, qf_ref, kf_ref, vf_ref, gf_scr, stf_scr,
                                      of_ref, tri_ref, msk_ref, diag_masks))
            streams.append(_gla_chunk(1, head, rb, qb_ref, kb_ref, vb_ref, gb_scr, stb_scr,
                                      ob_ref, tri_ref, msk_ref, diag_masks))
        _run_phase_major(streams, GLA_PHASES)
        return carry

    lax.fori_loop(0, n_chunks, body, 0, unroll=GLA_CHUNK_UNROLL)


def _gla(proj3, glr3, w2f, bf, w2b, bb, tri, msk, tb):
    b, s, _ = proj3.shape
    nt = s // tb
    qk_w, v_w = GLA_HEADS * GLA_DK, GLA_HEADS * GLA_DV
    fwd = lambda col: (lambda bi, t: (bi, t, col))
    bwd = lambda col: (lambda bi, t: (bi, nt - 1 - t, col))
    const2 = lambda bi, t: (0, 0)
    in_specs = []
    for mk in (fwd, bwd):
        in_specs += [
            pl.BlockSpec((1, tb, qk_w), mk(0)),
            pl.BlockSpec((1, tb, qk_w), mk(1)),
            pl.BlockSpec((1, tb, v_w), mk(1)),
            pl.BlockSpec((1, tb, 128), mk(0)),
        ]
    in_specs += [
        pl.BlockSpec((128, qk_w), const2), pl.BlockSpec((1, qk_w), const2),
        pl.BlockSpec((128, qk_w), const2), pl.BlockSpec((1, qk_w), const2),
        pl.BlockSpec(tri.shape, lambda bi, t: (0, 0, 0)),
        pl.BlockSpec(msk.shape, lambda bi, t: (0, 0, 0, 0)),
    ]
    return pl.pallas_call(
        _gla_kernel,
        out_shape=(jax.ShapeDtypeStruct((b, s, v_w), BF16),
                   jax.ShapeDtypeStruct((b, s, v_w), BF16)),
        grid=(b, nt),
        in_specs=in_specs,
        out_specs=(pl.BlockSpec((1, tb, v_w), fwd(0)),
                   pl.BlockSpec((1, tb, v_w), bwd(0))),
        scratch_shapes=[pltpu.VMEM((GLA_HEADS, GLA_DV, GLA_DK), F32),
                        pltpu.VMEM((GLA_HEADS, GLA_DV, GLA_DK), F32),
                        pltpu.VMEM((tb, qk_w), F32),
                        pltpu.VMEM((tb, qk_w), F32)],
        compiler_params=pltpu.CompilerParams(
            dimension_semantics=("arbitrary", "arbitrary"),
            vmem_limit_bytes=VMEM_LIMIT_BYTES),
        name="gla",
    )(proj3, proj3, proj3, glr3, proj3, proj3, proj3, glr3, w2f, bf, w2b, bb, tri, msk)


def _na_bias_table(rpb):
    col = np.arange(GRID_W)
    cs = np.clip(col - NA_COLS // 2, 0, GRID_W - NA_COLS)
    inside = (col[None, :] >= cs[:, None]) & (col[None, :] < cs[:, None] + NA_COLS)
    dc = col[None, :] - col[:, None] + (NA_COLS - 1)
    onehot = (np.arange(2 * NA_COLS - 1)[:, None, None] == dc[None]) & inside[None]
    t = jnp.einsum("hdc,cqk->hdqk", rpb.astype(F32), jnp.asarray(onehot, F32),
                   precision=lax.Precision.HIGHEST)
    t = jnp.where(inside[None, None], t, NEG_BIG)
    return jnp.concatenate([t[:, :-1], t[:, 1:]], axis=-1)


def _na_kernel(q_ref, k_ref, v_ref, t2_ref, o_ref, *, n_rows):
    rb = q_ref.shape[1] // GRID_W
    t = pl.program_id(1)
    gw = NA_GROUP * NA_HD
    head_of_lane = lax.broadcasted_iota(jnp.int32, (GRID_W, gw), 1) // NA_HD
    n_keys = NA_ROWS * GRID_W

    def pick(blocks):
        out = blocks[(NA_GROUP - 1) * GRID_W:]
        for h in range(NA_GROUP - 2, -1, -1):
            out = jnp.where(head_of_lane == h, blocks[h * GRID_W:(h + 1) * GRID_W], out)
        return out

    def body(rr, carry):
        r = t * rb + rr
        rs = jnp.clip(r - NA_ROWS // 2, 0, n_rows - NA_ROWS)
        off = rs - r + (NA_ROWS - 1)
        qrows = pl.ds(pl.multiple_of(rr * GRID_W, GRID_W), GRID_W)
        krows = pl.ds(pl.multiple_of(rs * GRID_W, GRID_W), n_keys)
        for grp in range(NA_HEADS // NA_GROUP):
            ls = slice(grp * gw, (grp + 1) * gw)
            q4 = q_ref[0, qrows, ls]
            k4 = k_ref[0, krows, ls]
            v4 = v_ref[0, krows, ls]
            lhs = jnp.concatenate(
                [jnp.where(head_of_lane == h, q4, jnp.zeros_like(q4)) for h in range(NA_GROUP)],
                axis=0)
            bias = jnp.concatenate(
                [jnp.concatenate([t2_ref[grp * NA_GROUP + h, off + 2 * w]
                                  for w in range(NA_ROWS // 2)], axis=1)
                 for h in range(NA_GROUP)], axis=0)
            s = _dot_nt(lhs, k4) + bias
            m = jnp.max(s, axis=-1, keepdims=True)
            pe = jnp.exp(s - m)
            den = jnp.sum(pe, axis=-1, keepdims=True)
            pv = _dot(pe.astype(BF16), v4)
            o_ref[0, qrows, ls] = (pick(pv) / pick(den)).astype(o_ref.dtype)
        return carry

    lax.fori_loop(0, rb, body, 0, unroll=True)


def _na(proj3, t2, rb):
    b, s, _ = proj3.shape
    n_rows = s // GRID_W
    w = NA_HEADS * NA_HD
    return pl.pallas_call(
        functools.partial(_na_kernel, n_rows=n_rows),
        out_shape=jax.ShapeDtypeStruct((b, s, w), BF16),
        grid=(b, n_rows // rb),
        in_specs=[
            pl.BlockSpec((1, rb * GRID_W, w), lambda bi, t: (bi, t, T_NQ)),
            pl.BlockSpec((1, s, w), lambda bi, t: (bi, 0, T_NK)),
            pl.BlockSpec((1, s, w), lambda bi, t: (bi, 0, T_NV)),
            pl.BlockSpec(t2.shape, lambda bi, t: (0, 0, 0, 0)),
        ],
        out_specs=pl.BlockSpec((1, rb * GRID_W, w), lambda bi, t: (bi, t, 0)),
        compiler_params=pltpu.CompilerParams(
            dimension_semantics=("arbitrary", "arbitrary"),
            vmem_limit_bytes=VMEM_LIMIT_BYTES),
        name="na",
    )(proj3, proj3, proj3, t2)


def _mem_kv_kernel(mem_ref, g_ref, w_ref, kg_ref, k_ref, v_ref):
    xf = mem_ref[0]
    ms = jnp.mean(xf * xf, axis=-1, keepdims=True)
    h = (xf * lax.rsqrt(ms + EPS) * g_ref[...]).astype(BF16)
    kv = _dot(h, w_ref[...])
    wk = MEM_HEADS * MEM_HD
    parts = []
    for hd in range(MEM_HEADS):
        kh = kv[:, hd * MEM_HD:(hd + 1) * MEM_HD]
        msk = jnp.mean(kh * kh, axis=-1, keepdims=True)
        parts.append(kh * lax.rsqrt(msk + EPS) * kg_ref[...])
    k_ref[0] = jnp.concatenate(parts, axis=1).astype(BF16)
    v_ref[0] = kv[:, wk:].astype(BF16)


def _mem_kv(mem, g, w, kg):
    b, m, d = mem.shape
    wk = MEM_HEADS * MEM_HD
    return pl.pallas_call(
        _mem_kv_kernel,
        out_shape=(jax.ShapeDtypeStruct((b, m, wk), BF16),
                   jax.ShapeDtypeStruct((b, m, wk), BF16)),
        grid=(b,),
        in_specs=[
            pl.BlockSpec((1, m, d), lambda bi: (bi, 0, 0)),
            pl.BlockSpec((1, d), lambda bi: (0, 0)),
            pl.BlockSpec((d, 2 * wk), lambda bi: (0, 0)),
            pl.BlockSpec((1, MEM_HD), lambda bi: (0, 0)),
        ],
        out_specs=(pl.BlockSpec((1, m, wk), lambda bi: (bi, 0, 0)),
                   pl.BlockSpec((1, m, wk), lambda bi: (bi, 0, 0))),
        compiler_params=pltpu.CompilerParams(
            dimension_semantics=("arbitrary",),
            vmem_limit_bytes=VMEM_LIMIT_BYTES),
        name="mem_kv",
    )(mem, g, w, kg)


def _merge_kernel(of_ref, ob_ref, ggate_ref, nb_ref, ngate_ref, mq_ref, mgate_ref, sig_ref,
                  x_ref, kc_ref, vc_ref, gout_ref, pa_ref, pb_ref, pc_ref, wo_ref, out_ref):
    d = x_ref.shape[1]
    o = of_ref[...].astype(F32) + ob_ref[...].astype(F32)
    parts = []
    for h in range(GLA_HEADS):
        oh = o[:, h * GLA_DV:(h + 1) * GLA_DV]
        ms = jnp.mean(oh * oh, axis=-1, keepdims=True)
        parts.append(oh * lax.rsqrt(ms + EPS) * gout_ref[...])
    oa = jnp.concatenate(parts, axis=1) * ggate_ref[...].astype(F32)
    ya = _dot(oa.astype(BF16), pa_ref[...])

    nb = nb_ref[...].astype(F32) * ngate_ref[...].astype(F32)
    yb = _dot(nb.astype(BF16), pb_ref[...])

    parts = []
    for h in range(MEM_HEADS):
        hs = slice(h * MEM_HD, (h + 1) * MEM_HD)
        s = _dot_nt(mq_ref[:, hs], kc_ref[0, :, hs])
        m = jnp.max(s, axis=-1, keepdims=True)
        pe = jnp.exp(s - m)
        den = jnp.sum(pe, axis=-1, keepdims=True)
        parts.append(_dot(pe.astype(BF16), vc_ref[0, :, hs]) / den)
    oc = jnp.concatenate(parts, axis=1) * mgate_ref[...].astype(F32)
    yc = _dot(oc.astype(BF16), pc_ref[...])

    y = (sig_ref[:, 0:d].astype(F32) * ya + sig_ref[:, d:2 * d].astype(F32) * yb
         + sig_ref[:, 2 * d:3 * d].astype(F32) * yc)
    out_ref[...] = x_ref[...] + _dot(y.astype(BF16), wo_ref[...])


def _merge(of2, ob2, nb2, proj2, x2, kc, vc, gout, pa, pb, pc, wo, tm, seq):
    m, d = x2.shape
    v_w = GLA_HEADS * GLA_DV
    na_w = NA_HEADS * NA_HD
    mem_w = MEM_HEADS * MEM_HD
    steps_per_batch = seq // tm
    row = lambda col: (lambda i: (i, col))
    const = lambda i: (0, 0)
    batch = lambda i: (i // steps_per_batch, 0, 0)
    return pl.pallas_call(
        _merge_kernel,
        out_shape=jax.ShapeDtypeStruct((m, d), F32),
        grid=(m // tm,),
        in_specs=[
            pl.BlockSpec((tm, v_w), row(0)),
            pl.BlockSpec((tm, v_w), row(0)),
            pl.BlockSpec((tm, v_w), row(T_GGATE * COL_TILE // v_w)),
            pl.BlockSpec((tm, na_w), row(0)),
            pl.BlockSpec((tm, na_w), row(T_NGATE)),
            pl.BlockSpec((tm, mem_w), row(T_MQ)),
            pl.BlockSpec((tm, mem_w), row(T_MGATE)),
            pl.BlockSpec((tm, 3 * d), row(T_MERGE * COL_TILE // (3 * d))),
            pl.BlockSpec((tm, d), row(0)),
            pl.BlockSpec((1,) + kc.shape[1:], batch),
            pl.BlockSpec((1,) + vc.shape[1:], batch),
            pl.BlockSpec((1, GLA_DV), const),
            pl.BlockSpec(pa.shape, const),
            pl.BlockSpec(pb.shape, const),
            pl.BlockSpec(pc.shape, const),
            pl.BlockSpec(wo.shape, const),
        ],
        out_specs=pl.BlockSpec((tm, d), row(0)),
        compiler_params=pltpu.CompilerParams(
            dimension_semantics=("arbitrary",),
            vmem_limit_bytes=VMEM_LIMIT_BYTES),
        name="merge",
    )(of2, ob2, proj2, nb2, proj2, proj2, proj2, proj2, x2, kc, vc, gout, pa, pb, pc, wo)


def _layer_params(l, norm_g, w_in, gla_w2_f, gla_b_f, gla_w2_b, gla_b_b, gla_out_g, p_a,
                  na_q_g, na_k_g, na_rpb, p_b, mem_norm_g, w_mem_kv, mem_q_g, mem_k_g, p_c,
                  w_out):
    d = w_in.shape[1]
    qk_w = GLA_HEADS * GLA_DK
    lr0 = 2 * qk_w + 2 * GLA_HEADS * GLA_DV
    lr1 = lr0 + 2 * GLA_RANK
    w = w_in[l]
    w_main = jnp.concatenate([w[:, :lr0], w[:, lr1:]], axis=1).astype(BF16)
    w_lr = jnp.pad(w[:, lr0:lr1], ((0, 0), (0, 128 - 2 * GLA_RANK))).astype(BF16)

    gains = jnp.ones((N_COL_TILES, COL_TILE), F32)
    gains = gains.at[T_GQ].set(GLA_DK ** -0.5)
    gains = gains.at[T_NQ].set(jnp.tile(na_q_g[l].astype(F32) * NA_HD ** -0.5, NA_HEADS))
    gains = gains.at[T_NK].set(jnp.tile(na_k_g[l].astype(F32), NA_HEADS))
    gains = gains.at[T_MQ].set(jnp.tile(mem_q_g[l].astype(F32) * MEM_HD ** -0.5, MEM_HEADS))

    w2f = jnp.zeros((128, qk_w), F32).at[:GLA_RANK].set(gla_w2_f[l]).astype(BF16)
    w2b = jnp.zeros((128, qk_w), F32).at[GLA_RANK:2 * GLA_RANK].set(gla_w2_b[l]).astype(BF16)
    return dict(
        norm_g=norm_g[l].reshape(1, d).astype(F32),
        w_main=w_main, w_lr=w_lr, gains=gains.reshape(N_COL_TILES, 1, COL_TILE),
        w2f=w2f, bf=gla_b_f[l].reshape(1, qk_w).astype(F32),
        w2b=w2b, bb=gla_b_b[l].reshape(1, qk_w).astype(F32),
        gout=gla_out_g[l].reshape(1, GLA_DV).astype(F32),
        t2=_na_bias_table(na_rpb[l]),
        mem_g=mem_norm_g[l].reshape(1, d).astype(F32),
        w_kv=w_mem_kv[l].astype(BF16),
        mem_kg=mem_k_g[l].reshape(1, MEM_HD).astype(F32),
        pa=p_a[l].astype(BF16), pb=p_b[l].astype(BF16), pc=p_c[l].astype(BF16),
        wo=w_out[l].astype(BF16),
    )


def _tiles(batch, seq):
    m = batch * seq
    tm_in = 2048 if m % 2048 == 0 else seq
    tm_merge = 512 if seq % 512 == 0 else seq
    tb_gla = 512 if seq % 512 == 0 else seq
    rb_na = 8
    return tm_in, tm_merge, tb_gla, rb_na


def kernel(x, mem, norm_g, w_in, gla_w2_f, gla_b_f, gla_w2_b, gla_b_b, gla_out_g, p_a, na_q_g,
           na_k_g, na_rpb, p_b, mem_norm_g, w_mem_kv, mem_q_g, mem_k_g, p_c, w_out):
    b, s, d = x.shape
    depth = w_in.shape[0]
    n_rows = s // GRID_W
    assert s % GRID_W == 0 and n_rows >= NA_ROWS and s % GLA_CHUNK == 0
    assert w_in.shape[2] - 2 * GLA_RANK == N_MAIN
    tm_in, tm_merge, tb_gla, rb_na = _tiles(b, s)
    tri, msk = _gla_constants()

    x2 = x.reshape(b * s, d).astype(F32)
    for l in range(depth):
        p = _layer_params(l, norm_g, w_in, gla_w2_f, gla_b_f, gla_w2_b, gla_b_b, gla_out_g,
                          p_a, na_q_g, na_k_g, na_rpb, p_b, mem_norm_g, w_mem_kv, mem_q_g,
                          mem_k_g, p_c, w_out)
        proj2, glr2 = _in_proj(x2, p["norm_g"], p["w_main"], p["w_lr"], p["gains"], tm_in)
        proj3 = proj2.reshape(b, s, N_MAIN)
        glr3 = glr2.reshape(b, s, 128)
        o_f, o_b = _gla(proj3, glr3, p["w2f"], p["bf"], p["w2b"], p["bb"], tri, msk, tb_gla)
        nb = _na(proj3, p["t2"], rb_na)
        kc, vc = _mem_kv(mem.astype(F32), p["mem_g"], p["w_kv"], p["mem_kg"])
        x2 = _merge(o_f.reshape(b * s, -1), o_b.reshape(b * s, -1), nb.reshape(b * s, -1),
                    proj2, x2, kc, vc, p["gout"], p["pa"], p["pb"], p["pc"], p["wo"],
                    tm_merge, s)
    return x2.reshape(b, s, d).astype(x.dtype)
```

```python
import functools

import numpy as np
import jax
import jax.numpy as jnp
from jax import lax
from jax.experimental import pallas as pl
from jax.experimental.pallas import tpu as pltpu

F32 = jnp.float32
BF16 = jnp.bfloat16

EPS = 1e-6
NEG_BIG = -1e30
LOG2_E = 1.4426950408889634

GRID_W = 64
GLA_HEADS, GLA_DK, GLA_DV = 4, 128, 256
GLA_RANK = 16
GLA_TAU = 16.0
GLA_CHUNK = 64
GLA_LEVELS = (8, 16, 32, 64)
GLA_PHASES = 4
GLA_CHUNK_UNROLL = 4
NA_HEADS, NA_HD = 8, 64
NA_ROWS, NA_COLS = 8, 16
NA_GROUP = 4
MEM_HEADS, MEM_HD = 4, 128

COL_TILE = 512
N_MAIN = 9216
T_GQ, T_GK, T_GV, T_GGATE = 0, 1, 2, 4
T_NQ, T_NK, T_NV, T_NGATE = 6, 7, 8, 9
T_MQ, T_MGATE, T_MERGE = 10, 11, 12
N_COL_TILES = N_MAIN // COL_TILE
X_PIECES = 4
IN_PROJ_SUB_ROWS = 128

VMEM_LIMIT_BYTES = 48 * 1024 * 1024


def _dot(a, b):
    return jnp.dot(a, b, preferred_element_type=F32)


def _dot_nt(a, b):
    return lax.dot_general(a, b, (((1,), (1,)), ((), ())), preferred_element_type=F32)


def _dot_tn(a, b):
    return lax.dot_general(a, b, (((0,), (0,)), ((), ())), preferred_element_type=F32)


def _in_proj_kernel(*refs):
    x_refs = refs[:X_PIECES]
    g_ref, w_ref, wlr_ref, gain_ref, proj_ref, glr_ref, h_ref = refs[X_PIECES:]
    j = pl.program_id(1)

    @pl.when(j == 0)
    def _():
        xf = jnp.concatenate([x_ref[...] for x_ref in x_refs], axis=1)
        ms = jnp.mean(xf * xf, axis=-1, keepdims=True)
        h = (xf * lax.rsqrt(ms + EPS) * g_ref[...]).astype(BF16)
        h_ref[...] = h
        glr_ref[...] = _dot(h, wlr_ref[...])

    gain = gain_ref[0]

    def tile(epilogue):
        for r0 in range(0, h_ref.shape[0], IN_PROJ_SUB_ROWS):
            rows = slice(r0, r0 + IN_PROJ_SUB_ROWS)
            acc = _dot(h_ref[rows, :], w_ref[...])
            proj_ref[rows, :] = epilogue(acc).astype(BF16)

    def normed(head_w):
        def epilogue(acc):
            sq = acc * acc
            lane = lax.broadcasted_iota(jnp.int32, (acc.shape[0], 128), 1)
            groups = []
            for c0 in range(0, COL_TILE, 128):
                sq_g = sq[:, c0:c0 + 128]
                scale = None
                for h0 in range(0, 128, head_w):
                    mine = (lane >= h0) & (lane < h0 + head_w)
                    ms = jnp.sum(jnp.where(mine, sq_g, 0.0), axis=-1, keepdims=True) * (1.0 / head_w)
                    r = lax.rsqrt(ms + EPS)
                    scale = r if scale is None else jnp.where(mine, r, scale)
                groups.append(jnp.broadcast_to(scale, sq_g.shape))
            return acc * jnp.concatenate(groups, axis=1) * gain
        return epilogue

    is_silu = (j == T_GGATE) | (j == T_GGATE + 1) | (j == T_NGATE) | (j == T_MGATE)
    is_sig = j >= T_MERGE
    is_n64 = (j == T_NQ) | (j == T_NK)
    is_n128 = j == T_MQ
    is_lin = jnp.logical_not(is_silu | is_sig | is_n64 | is_n128)

    pl.when(is_lin)(lambda: tile(lambda acc: acc * gain))
    pl.when(is_silu)(lambda: tile(lambda acc: acc * jax.nn.sigmoid(acc)))
    pl.when(is_sig)(lambda: tile(jax.nn.sigmoid))
    pl.when(is_n64)(lambda: tile(normed(NA_HD)))
    pl.when(is_n128)(lambda: tile(normed(MEM_HD)))


def _in_proj(x2, g, w_main, w_lr, gains, layer, tm):
    m, d = x2.shape
    n_row_tiles = m // tm

    def x_piece(p):
        switch = N_COL_TILES - X_PIECES + p
        return pl.BlockSpec(
            (tm, d // X_PIECES),
            lambda i, j: (jnp.minimum(i + (j > switch).astype(jnp.int32), n_row_tiles - 1), p))

    return pl.pallas_call(
        _in_proj_kernel,
        out_shape=(jax.ShapeDtypeStruct((m, N_MAIN), BF16),
                   jax.ShapeDtypeStruct((m, 128), F32)),
        grid=(m // tm, N_COL_TILES),
        in_specs=[x_piece(p) for p in range(X_PIECES)] + [
            pl.BlockSpec((1, d), lambda i, j: (0, 0)),
            pl.BlockSpec((None, d, COL_TILE), lambda i, j: (layer, 0, j)),
            pl.BlockSpec((None, d, 128), lambda i, j: (layer, 0, 0)),
            pl.BlockSpec((1, 1, COL_TILE), lambda i, j: (j, 0, 0)),
        ],
        out_specs=(pl.BlockSpec((tm, COL_TILE), lambda i, j: (i, j)),
                   pl.BlockSpec((tm, 128), lambda i, j: (i, 0))),
        scratch_shapes=[pltpu.VMEM((tm, d), BF16)],
        compiler_params=pltpu.CompilerParams(
            dimension_semantics=("arbitrary", "arbitrary"),
            vmem_limit_bytes=VMEM_LIMIT_BYTES),
        name="in_proj",
    )(*([x2] * X_PIECES), g, w_main, w_lr, gains)


def _run_phase_major(chains, n_phases):
    for _ in range(n_phases):
        for chain in chains:
            next(chain)


def _gla_constants():
    c = GLA_CHUNK
    idx = np.arange(c)
    tri = np.zeros((2, 2 * len(GLA_LEVELS) * c, c), np.float32)
    msk = np.zeros((2, len(GLA_LEVELS), c, c), np.float32)
    for d in range(2):
        covered = np.zeros((c, c), bool)
        for n, l in enumerate(GLA_LEVELS):
            blk = idx // l
            same = blk[:, None] == blk[None, :]
            if d == 0:
                wq = same & (idx[None, :] <= idx[:, None])
                wk = same & (idx[None, :] > idx[:, None])
                adj = blk[:, None] == blk[None, :] + 1
            else:
                wq = same & (idx[None, :] >= idx[:, None])
                wk = same & (idx[None, :] < idx[:, None])
                adj = blk[None, :] == blk[:, None] + 1
            tri[d, n * c:(n + 1) * c] = wq
            tri[d, (len(GLA_LEVELS) + n) * c:(len(GLA_LEVELS) + n + 1) * c] = wk
            if l != c:
                m = adj & ~covered
                covered |= m
                msk[d, n] = m
    tri = np.concatenate([tri, tri], axis=2)
    return jnp.asarray(tri, BF16), jnp.asarray(msk[:, :len(GLA_LEVELS) - 1], F32)


def _log_sigmoid(z):
    return jnp.minimum(z, 0.0) - jnp.log(1.0 + jnp.exp(-jnp.abs(z)))


def _gla_chunk(d, head, r0, q_ref, k_ref, v_ref, g_scr, st_scr, o_ref, tri_ref, msk_ref,
               diag_masks):
    c, nl = GLA_CHUNK, len(GLA_LEVELS)
    ks = slice(head * GLA_DK, (head + 1) * GLA_DK)
    vs = slice(head * GLA_DV, (head + 1) * GLA_DV)
    rows = pl.ds(r0, c)
    q = q_ref[0, rows, ks].astype(F32)
    k = k_ref[0, rows, ks].astype(F32)
    v = v_ref[0, rows, vs]
    g = g_scr[rows, ks]

    g_hi = g.astype(BF16)
    g_lo = (g - g_hi.astype(F32)).astype(BF16)
    sums = _dot(tri_ref[d], jnp.concatenate([g_hi, g_lo], axis=0))
    yield
    e = jnp.exp2(sums)

    def lvl(n):
        eq = e[n * c:(n + 1) * c]
        ek = e[(nl + n) * c:(nl + n + 1) * c]
        return (q * eq).astype(BF16), (k * ek).astype(BF16)

    lvl_dots = []
    for n in range(nl - 1):
        qn, kn = lvl(n)
        lvl_dots.append(_dot_nt(qn, kn))
    q64, k64 = lvl(nl - 1)
    st = st_scr[head]
    o_inter = _dot_nt(q64, st.astype(BF16))
    st_update = _dot_tn(v, k64)
    yield

    blocks = lambda x: x.reshape(c // 8, 8, GLA_DK)
    step = 1 if d == 0 else 7
    q3, kd, decay = blocks(q), blocks(k), blocks(jnp.exp2(g))
    a = jnp.where(diag_masks[d][0], jnp.sum(q * k, axis=-1, keepdims=True), 0.0)
    for lag in range(1, 8):
        kd = decay * pltpu.roll(kd, step, 1)
        col = jnp.sum(q3 * kd, axis=-1, keepdims=True).reshape(c, 1)
        a = jnp.where(diag_masks[d][lag], col, a)
    yield

    for n in range(nl - 1):
        a = a + msk_ref[d, n] * lvl_dots[n]
    o = _dot(a.astype(BF16), v) + o_inter
    o_ref[0, rows, vs] = o.astype(o_ref.dtype)

    tot_row = (nl - 1) * c + (c - 1 if d == 0 else 0)
    st_scr[head] = st * e[tot_row:tot_row + 1] + st_update
    yield


def _gla_kernel(qf_ref, kf_ref, vf_ref, lrf_ref, qb_ref, kb_ref, vb_ref, lrb_ref,
                w2f_ref, bf_ref, w2b_ref, bb_ref, tri_ref, msk_ref,
                of_ref, ob_ref, stf_scr, stb_scr, gf_scr, gb_scr):
    t = pl.program_id(1)

    @pl.when(t == 0)
    def _():
        stf_scr[...] = jnp.zeros_like(stf_scr)
        stb_scr[...] = jnp.zeros_like(stb_scr)

    to_log2_decay = LOG2_E / GLA_TAU
    zf = _dot(lrf_ref[0].astype(BF16), w2f_ref[...]) + bf_ref[...]
    gf_scr[...] = _log_sigmoid(zf) * to_log2_decay
    zb = _dot(lrb_ref[0].astype(BF16), w2b_ref[...]) + bb_ref[...]
    gb_scr[...] = _log_sigmoid(zb) * to_log2_decay

    n_chunks = qf_ref.shape[1] // GLA_CHUNK

    row = lax.broadcasted_iota(jnp.int32, (GLA_CHUNK, GLA_CHUNK), 0)
    col = lax.broadcasted_iota(jnp.int32, (GLA_CHUNK, GLA_CHUNK), 1)
    same_block = (row // 8) == (col // 8)
    diag_masks = [[same_block & (col == row - lag) for lag in range(8)],
                  [same_block & (col == row + lag) for lag in range(8)]]

    def body(ci, carry):
        rf = pl.multiple_of(ci * GLA_CHUNK, GLA_CHUNK)
        rb = pl.multiple_of((n_chunks - 1 - ci) * GLA_CHUNK, GLA_CHUNK)
        streams = []
        for head in range(GLA_HEADS):
            streams.append(_gla_chunk(0, head, rf, qf_ref, kf_ref, vf_ref, gf_scr, stf_scr,
                                      of_ref, tri_ref, msk_ref, diag_masks))
            streams.append(_gla_chunk(1, head, rb, qb_ref, kb_ref, vb_ref, gb_scr, stb_scr,
                                      ob_ref, tri_ref, msk_ref, diag_masks))
        _run_phase_major(streams, GLA_PHASES)
        return carry

    lax.fori_loop(0, n_chunks, body, 0, unroll=GLA_CHUNK_UNROLL)


def _gla(proj3, glr3, w2f, bf, w2b, bb, tri, msk, tb):
    b, s, _ = proj3.shape
    nt = s // tb
    qk_w, v_w = GLA_HEADS * GLA_DK, GLA_HEADS * GLA_DV
    fwd = lambda col: (lambda bi, t: (bi, t, col))
    bwd = lambda col: (lambda bi, t: (bi, nt - 1 - t, col))
    const2 = lambda bi, t: (0, 0)
    in_specs = []
    for mk in (fwd, bwd):
        in_specs += [
            pl.BlockSpec((1, tb, qk_w), mk(0)),
            pl.BlockSpec((1, tb, qk_w), mk(1)),
            pl.BlockSpec((1, tb, v_w), mk(1)),
            pl.BlockSpec((1, tb, 128), mk(0)),
        ]
    in_specs += [
        pl.BlockSpec((128, qk_w), const2), pl.BlockSpec((1, qk_w), const2),
        pl.BlockSpec((128, qk_w), const2), pl.BlockSpec((1, qk_w), const2),
        pl.BlockSpec(tri.shape, lambda bi, t: (0, 0, 0)),
        pl.BlockSpec(msk.shape, lambda bi, t: (0, 0, 0, 0)),
    ]
    return pl.pallas_call(
        _gla_kernel,
        out_shape=(jax.ShapeDtypeStruct((b, s, v_w), BF16),
                   jax.ShapeDtypeStruct((b, s, v_w), BF16)),
        grid=(b, nt),
        in_specs=in_specs,
        out_specs=(pl.BlockSpec((1, tb, v_w), fwd(0)),
                   pl.BlockSpec((1, tb, v_w), bwd(0))),
        scratch_shapes=[pltpu.VMEM((GLA_HEADS, GLA_DV, GLA_DK), F32),
                        pltpu.VMEM((GLA_HEADS, GLA_DV, GLA_DK), F32),
                        pltpu.VMEM((tb, qk_w), F32),
                        pltpu.VMEM((tb, qk_w), F32)],
        compiler_params=pltpu.CompilerParams(
            dimension_semantics=("arbitrary", "arbitrary"),
            vmem_limit_bytes=VMEM_LIMIT_BYTES),
        name="gla",
    )(proj3, proj3, proj3, glr3, proj3, proj3, proj3, glr3, w2f, bf, w2b, bb, tri, msk)


def _na_bias_table(rpb):
    col = np.arange(GRID_W)
    cs = np.clip(col - NA_COLS // 2, 0, GRID_W - NA_COLS)
    inside = (col[None, :] >= cs[:, None]) & (col[None, :] < cs[:, None] + NA_COLS)
    dc = col[None, :] - col[:, None] + (NA_COLS - 1)
    onehot = (np.arange(2 * NA_COLS - 1)[:, None, None] == dc[None]) & inside[None]
    t = jnp.einsum("lhdc,cqk->lhdqk", rpb.astype(F32), jnp.asarray(onehot, F32),
                   precision=lax.Precision.HIGHEST)
    t = jnp.where(inside[None, None, None], t, NEG_BIG)
    return jnp.concatenate([t[:, :, :-1], t[:, :, 1:]], axis=-1)


def _na_kernel(q_ref, k_ref, v_ref, t2_ref, o_ref, *, n_rows):
    rb = q_ref.shape[1] // GRID_W
    t = pl.program_id(1)
    gw = NA_GROUP * NA_HD
    head_of_lane = lax.broadcasted_iota(jnp.int32, (GRID_W, gw), 1) // NA_HD
    n_keys = NA_ROWS * GRID_W

    def pick(blocks):
        out = blocks[(NA_GROUP - 1) * GRID_W:]
        for h in range(NA_GROUP - 2, -1, -1):
            out = jnp.where(head_of_lane == h, blocks[h * GRID_W:(h + 1) * GRID_W], out)
        return out

    def body(rr, carry):
        r = t * rb + rr
        rs = jnp.clip(r - NA_ROWS // 2, 0, n_rows - NA_ROWS)
        off = rs - r + (NA_ROWS - 1)
        qrows = pl.ds(pl.multiple_of(rr * GRID_W, GRID_W), GRID_W)
        krows = pl.ds(pl.multiple_of(rs * GRID_W, GRID_W), n_keys)
        for grp in range(NA_HEADS // NA_GROUP):
            ls = slice(grp * gw, (grp + 1) * gw)
            q4 = q_ref[0, qrows, ls]
            k4 = k_ref[0, krows, ls]
            v4 = v_ref[0, krows, ls]
            lhs = jnp.concatenate(
                [jnp.where(head_of_lane == h, q4, jnp.zeros_like(q4)) for h in range(NA_GROUP)],
                axis=0)
            bias = jnp.concatenate(
                [jnp.concatenate([t2_ref[grp * NA_GROUP + h, off + 2 * w]
                                  for w in range(NA_ROWS // 2)], axis=1)
                 for h in range(NA_GROUP)], axis=0)
            s = _dot_nt(lhs, k4) + bias
            m = jnp.max(s, axis=-1, keepdims=True)
            pe = jnp.exp(s - m)
            den = jnp.sum(pe, axis=-1, keepdims=True)
            pv = _dot(pe.astype(BF16), v4)
            o_ref[0, qrows, ls] = (pick(pv) / pick(den)).astype(o_ref.dtype)
        return carry

    lax.fori_loop(0, rb, body, 0, unroll=True)


def _na(proj3, t2, layer, rb):
    b, s, _ = proj3.shape
    n_rows = s // GRID_W
    w = NA_HEADS * NA_HD
    return pl.pallas_call(
        functools.partial(_na_kernel, n_rows=n_rows),
        out_shape=jax.ShapeDtypeStruct((b, s, w), BF16),
        grid=(b, n_rows // rb),
        in_specs=[
            pl.BlockSpec((1, rb * GRID_W, w), lambda bi, t: (bi, t, T_NQ)),
            pl.BlockSpec((1, s, w), lambda bi, t: (bi, 0, T_NK)),
            pl.BlockSpec((1, s, w), lambda bi, t: (bi, 0, T_NV)),
            pl.BlockSpec((None,) + t2.shape[1:], lambda bi, t: (layer, 0, 0, 0, 0)),
        ],
        out_specs=pl.BlockSpec((1, rb * GRID_W, w), lambda bi, t: (bi, t, 0)),
        compiler_params=pltpu.CompilerParams(
            dimension_semantics=("arbitrary", "arbitrary"),
            vmem_limit_bytes=VMEM_LIMIT_BYTES),
        name="na",
    )(proj3, proj3, proj3, t2)


def _mem_kv_kernel(mem_ref, g_ref, w_ref, kg_ref, k_ref, v_ref):
    xf = mem_ref[0]
    ms = jnp.mean(xf * xf, axis=-1, keepdims=True)
    h = (xf * lax.rsqrt(ms + EPS) * g_ref[...]).astype(BF16)
    kv = _dot(h, w_ref[...])
    wk = MEM_HEADS * MEM_HD
    parts = []
    for hd in range(MEM_HEADS):
        kh = kv[:, hd * MEM_HD:(hd + 1) * MEM_HD]
        msk = jnp.mean(kh * kh, axis=-1, keepdims=True)
        parts.append(kh * lax.rsqrt(msk + EPS) * kg_ref[...])
    k_ref[0] = jnp.concatenate(parts, axis=1).astype(BF16)
    v_ref[0] = kv[:, wk:].astype(BF16)


def _mem_kv(mem, g, w, kg, layer):
    b, m, d = mem.shape
    wk = MEM_HEADS * MEM_HD
    return pl.pallas_call(
        _mem_kv_kernel,
        out_shape=(jax.ShapeDtypeStruct((b, m, wk), BF16),
                   jax.ShapeDtypeStruct((b, m, wk), BF16)),
        grid=(b,),
        in_specs=[
            pl.BlockSpec((1, m, d), lambda bi: (bi, 0, 0)),
            pl.BlockSpec((1, d), lambda bi: (0, 0)),
            pl.BlockSpec((None, d, 2 * wk), lambda bi: (layer, 0, 0)),
            pl.BlockSpec((1, MEM_HD), lambda bi: (0, 0)),
        ],
        out_specs=(pl.BlockSpec((1, m, wk), lambda bi: (bi, 0, 0)),
                   pl.BlockSpec((1, m, wk), lambda bi: (bi, 0, 0))),
        compiler_params=pltpu.CompilerParams(
            dimension_semantics=("arbitrary",),
            vmem_limit_bytes=VMEM_LIMIT_BYTES),
        name="mem_kv",
    )(mem, g, w, kg)


def _merge_kernel(of_ref, ob_ref, ggate_ref, nb_ref, ngate_ref, mq_ref, mgate_ref, sig_ref,
                  x_ref, kc_ref, vc_ref, gout_ref, pa_ref, pb_ref, pc_ref, wo_ref, out_ref):
    d = x_ref.shape[1]
    o = of_ref[...].astype(F32) + ob_ref[...].astype(F32)
    parts = []
    for h in range(GLA_HEADS):
        oh = o[:, h * GLA_DV:(h + 1) * GLA_DV]
        ms = jnp.mean(oh * oh, axis=-1, keepdims=True)
        parts.append(oh * lax.rsqrt(ms + EPS) * gout_ref[...])
    oa = jnp.concatenate(parts, axis=1) * ggate_ref[...].astype(F32)
    ya = _dot(oa.astype(BF16), pa_ref[...])

    nb = nb_ref[...].astype(F32) * ngate_ref[...].astype(F32)
    yb = _dot(nb.astype(BF16), pb_ref[...])

    parts = []
    for h in range(MEM_HEADS):
        hs = slice(h * MEM_HD, (h + 1) * MEM_HD)
        s = _dot_nt(mq_ref[:, hs], kc_ref[0, :, hs])
        m = jnp.max(s, axis=-1, keepdims=True)
        pe = jnp.exp(s - m)
        den = jnp.sum(pe, axis=-1, keepdims=True)
        parts.append(_dot(pe.astype(BF16), vc_ref[0, :, hs]) / den)
    oc = jnp.concatenate(parts, axis=1) * mgate_ref[...].astype(F32)
    yc = _dot(oc.astype(BF16), pc_ref[...])

    y = (sig_ref[:, 0:d].astype(F32) * ya + sig_ref[:, d:2 * d].astype(F32) * yb
         + sig_ref[:, 2 * d:3 * d].astype(F32) * yc)
    out_ref[...] = x_ref[...] + _dot(y.astype(BF16), wo_ref[...])


def _merge(of2, ob2, nb2, proj2, x2, kc, vc, gout, pa, pb, pc, wo, layer, tm, seq):
    m, d = x2.shape
    v_w = GLA_HEADS * GLA_DV
    na_w = NA_HEADS * NA_HD
    mem_w = MEM_HEADS * MEM_HD
    steps_per_batch = seq // tm
    row = lambda col: (lambda i: (i, col))
    const = lambda i: (0, 0)
    weight = lambda w: pl.BlockSpec((None,) + w.shape[1:], lambda i: (layer, 0, 0))
    batch = lambda i: (i // steps_per_batch, 0, 0)
    return pl.pallas_call(
        _merge_kernel,
        out_shape=jax.ShapeDtypeStruct((m, d), F32),
        grid=(m // tm,),
        in_specs=[
            pl.BlockSpec((tm, v_w), row(0)),
            pl.BlockSpec((tm, v_w), row(0)),
            pl.BlockSpec((tm, v_w), row(T_GGATE * COL_TILE // v_w)),
            pl.BlockSpec((tm, na_w), row(0)),
            pl.BlockSpec((tm, na_w), row(T_NGATE)),
            pl.BlockSpec((tm, mem_w), row(T_MQ)),
            pl.BlockSpec((tm, mem_w), row(T_MGATE)),
            pl.BlockSpec((tm, 3 * d), row(T_MERGE * COL_TILE // (3 * d))),
            pl.BlockSpec((tm, d), row(0)),
            pl.BlockSpec((1,) + kc.shape[1:], batch),
            pl.BlockSpec((1,) + vc.shape[1:], batch),
            pl.BlockSpec((1, GLA_DV), const),
            weight(pa), weight(pb), weight(pc), weight(wo),
        ],
        out_specs=pl.BlockSpec((tm, d), row(0)),
        compiler_params=pltpu.CompilerParams(
            dimension_semantics=("arbitrary",),
            vmem_limit_bytes=VMEM_LIMIT_BYTES),
        name="merge",
    )(of2, ob2, proj2, nb2, proj2, proj2, proj2, proj2, x2, kc, vc, gout, pa, pb, pc, wo)


def _stacked_weights(w_in, na_rpb, w_mem_kv, p_a, p_b, p_c, w_out):
    qk_w = GLA_HEADS * GLA_DK
    lr0 = 2 * qk_w + 2 * GLA_HEADS * GLA_DV
    lr1 = lr0 + 2 * GLA_RANK
    w_main = jnp.concatenate([w_in[:, :, :lr0], w_in[:, :, lr1:]], axis=2).astype(BF16)
    w_lr = jnp.pad(w_in[:, :, lr0:lr1], ((0, 0), (0, 0), (0, 128 - 2 * GLA_RANK))).astype(BF16)
    return dict(w_main=w_main, w_lr=w_lr, t2=_na_bias_table(na_rpb),
                w_kv=w_mem_kv.astype(BF16), pa=p_a.astype(BF16), pb=p_b.astype(BF16),
                pc=p_c.astype(BF16), wo=w_out.astype(BF16))


def _layer_params(l, norm_g, gla_w2_f, gla_b_f, gla_w2_b, gla_b_b, gla_out_g, na_q_g, na_k_g,
                  mem_norm_g, mem_q_g, mem_k_g):
    d = norm_g.shape[1]
    qk_w = GLA_HEADS * GLA_DK
    gains = jnp.ones((N_COL_TILES, COL_TILE), F32)
    gains = gains.at[T_GQ].set(GLA_DK ** -0.5)
    gains = gains.at[T_NQ].set(jnp.tile(na_q_g[l].astype(F32) * NA_HD ** -0.5, NA_HEADS))
    gains = gains.at[T_NK].set(jnp.tile(na_k_g[l].astype(F32), NA_HEADS))
    gains = gains.at[T_MQ].set(jnp.tile(mem_q_g[l].astype(F32) * MEM_HD ** -0.5, MEM_HEADS))

    w2f = jnp.zeros((128, qk_w), F32).at[:GLA_RANK].set(gla_w2_f[l]).astype(BF16)
    w2b = jnp.zeros((128, qk_w), F32).at[GLA_RANK:2 * GLA_RANK].set(gla_w2_b[l]).astype(BF16)
    return dict(
        norm_g=norm_g[l].reshape(1, d).astype(F32),
        gains=gains.reshape(N_COL_TILES, 1, COL_TILE),
        w2f=w2f, bf=gla_b_f[l].reshape(1, qk_w).astype(F32),
        w2b=w2b, bb=gla_b_b[l].reshape(1, qk_w).astype(F32),
        gout=gla_out_g[l].reshape(1, GLA_DV).astype(F32),
        mem_g=mem_norm_g[l].reshape(1, d).astype(F32),
        mem_kg=mem_k_g[l].reshape(1, MEM_HD).astype(F32),
    )


def _tiles(batch, seq):
    m = batch * seq
    tm_in = 2048 if m % 2048 == 0 else seq
    tm_merge = 512 if seq % 512 == 0 else seq
    tb_gla = 512 if seq % 512 == 0 else seq
    rb_na = 8
    return tm_in, tm_merge, tb_gla, rb_na


def kernel(x, mem, norm_g, w_in, gla_w2_f, gla_b_f, gla_w2_b, gla_b_b, gla_out_g, p_a, na_q_g,
           na_k_g, na_rpb, p_b, mem_norm_g, w_mem_kv, mem_q_g, mem_k_g, p_c, w_out):
    b, s, d = x.shape
    depth = w_in.shape[0]
    n_rows = s // GRID_W
    assert s % GRID_W == 0 and n_rows >= NA_ROWS and s % GLA_CHUNK == 0
    assert w_in.shape[2] - 2 * GLA_RANK == N_MAIN
    tm_in, tm_merge, tb_gla, rb_na = _tiles(b, s)
    tri, msk = _gla_constants()

    big = _stacked_weights(w_in, na_rpb, w_mem_kv, p_a, p_b, p_c, w_out)
    x2 = x.reshape(b * s, d).astype(F32)
    for l in range(depth):
        p = _layer_params(l, norm_g, gla_w2_f, gla_b_f, gla_w2_b, gla_b_b, gla_out_g, na_q_g,
                          na_k_g, mem_norm_g, mem_q_g, mem_k_g)
        proj2, glr2 = _in_proj(x2, p["norm_g"], big["w_main"], big["w_lr"], p["gains"], l, tm_in)
        proj3 = proj2.reshape(b, s, N_MAIN)
        glr3 = glr2.reshape(b, s, 128)
        o_f, o_b = _gla(proj3, glr3, p["w2f"], p["bf"], p["w2b"], p["bb"], tri, msk, tb_gla)
        nb = _na(proj3, big["t2"], l, rb_na)
        kc, vc = _mem_kv(mem.astype(F32), p["mem_g"], big["w_kv"], p["mem_kg"], l)
        x2 = _merge(o_f.reshape(b * s, -1), o_b.reshape(b * s, -1), nb.reshape(b * s, -1),
                    proj2, x2, kc, vc, p["gout"], big["pa"], big["pb"], big["pc"], big["wo"],
                    l, tm_merge, s)
    return x2.reshape(b, s, d).astype(x.dtype)
```

```python
import functools

import numpy as np
import jax
import jax.numpy as jnp
from jax import lax
from jax.experimental import pallas as pl
from jax.experimental.pallas import tpu as pltpu

F32 = jnp.float32
BF16 = jnp.bfloat16

EPS = 1e-6
NEG_BIG = -1e30
LOG2_E = 1.4426950408889634

GRID_W = 64
GLA_HEADS, GLA_DK, GLA_DV = 4, 128, 256
GLA_RANK = 16
GLA_TAU = 16.0
GLA_CHUNK = 64
GLA_LEVELS = (8, 16, 32, 64)
GLA_PHASES = 4
GLA_CHUNK_UNROLL = 4
NA_HEADS, NA_HD = 8, 64
NA_ROWS, NA_COLS = 8, 16
NA_GROUP = 4
MEM_HEADS, MEM_HD = 4, 128

COL_TILE = 512
N_MAIN = 9216
T_GQ, T_GK, T_GV, T_GGATE = 0, 1, 2, 4
T_NQ, T_NK, T_NV, T_NGATE = 6, 7, 8, 9
T_MQ, T_MGATE, T_MERGE = 10, 11, 12
N_COL_TILES = N_MAIN // COL_TILE
IN_PROJ_WIDE_COLS = 2 * COL_TILE
X_PIECES = 4
IN_PROJ_SUB_ROWS = 256

VMEM_LIMIT_BYTES = 48 * 1024 * 1024


def _dot(a, b):
    return jnp.dot(a, b, preferred_element_type=F32)


def _dot_nt(a, b):
    return lax.dot_general(a, b, (((1,), (1,)), ((), ())), preferred_element_type=F32)


def _dot_tn(a, b):
    return lax.dot_general(a, b, (((0,), (0,)), ((), ())), preferred_element_type=F32)


def _col_kinds():
    kinds = ["lin"] * N_COL_TILES
    for t in (T_GGATE, T_GGATE + 1, T_NGATE, T_MGATE):
        kinds[t] = "silu"
    kinds[T_NQ] = kinds[T_NK] = "n64"
    kinds[T_MQ] = "n128"
    kinds[T_MERGE:] = ["sig"] * (N_COL_TILES - T_MERGE)
    return kinds


def _in_proj_kernel(*refs):
    x_refs = refs[:X_PIECES]
    g_ref, w_ref, wlr_ref, gain_ref, proj_ref, glr_ref, h_ref = refs[X_PIECES:]
    j = pl.program_id(1)

    @pl.when(j == 0)
    def _():
        xf = jnp.concatenate([x_ref[...] for x_ref in x_refs], axis=1)
        ms = jnp.mean(xf * xf, axis=-1, keepdims=True)
        h = (xf * lax.rsqrt(ms + EPS) * g_ref[...]).astype(BF16)
        h_ref[...] = h
        glr_ref[...] = _dot(h, wlr_ref[...])

    gain = gain_ref[0]

    def normed(head_w):
        def epilogue(acc, gain_cols):
            sq = acc * acc
            lane = lax.broadcasted_iota(jnp.int32, (acc.shape[0], 128), 1)
            groups = []
            for c0 in range(0, COL_TILE, 128):
                sq_g = sq[:, c0:c0 + 128]
                scale = None
                for h0 in range(0, 128, head_w):
                    mine = (lane >= h0) & (lane < h0 + head_w)
                    ms = jnp.sum(jnp.where(mine, sq_g, 0.0), axis=-1, keepdims=True) * (1.0 / head_w)
                    r = lax.rsqrt(ms + EPS)
                    scale = r if scale is None else jnp.where(mine, r, scale)
                groups.append(jnp.broadcast_to(scale, sq_g.shape))
            return acc * jnp.concatenate(groups, axis=1) * gain_cols
        return epilogue

    epilogues = {
        "lin": lambda acc, gain_cols: acc * gain_cols,
        "silu": lambda acc, gain_cols: acc * jax.nn.sigmoid(acc),
        "sig": lambda acc, gain_cols: jax.nn.sigmoid(acc),
        "n64": normed(NA_HD),
        "n128": normed(MEM_HD),
    }

    def tile(kinds):
        for r0 in range(0, h_ref.shape[0], IN_PROJ_SUB_ROWS):
            rows = slice(r0, r0 + IN_PROJ_SUB_ROWS)
            acc = _dot(h_ref[rows, :], w_ref[...])
            for n, kind in enumerate(kinds):
                cols = slice(n * COL_TILE, (n + 1) * COL_TILE)
                proj_ref[rows, cols] = epilogues[kind](acc[:, cols], gain[:, cols]).astype(BF16)

    col_kinds = _col_kinds()
    wide = IN_PROJ_WIDE_COLS // COL_TILE
    combos = {}
    for jt in range(N_COL_TILES // wide):
        combos.setdefault(tuple(col_kinds[jt * wide:(jt + 1) * wide]), []).append(jt)
    for kinds, tiles in combos.items():
        cond = j == tiles[0]
        for jt in tiles[1:]:
            cond = cond | (j == jt)
        pl.when(cond)(functools.partial(tile, kinds))


def _in_proj(x2, g, w_main, w_lr, gains, layer, tm):
    m, d = x2.shape
    n_row_tiles = m // tm

    def x_piece(p):
        switch = N_MAIN // IN_PROJ_WIDE_COLS - X_PIECES + p
        return pl.BlockSpec(
            (tm, d // X_PIECES),
            lambda i, j: (jnp.minimum(i + (j > switch).astype(jnp.int32), n_row_tiles - 1), p))

    return pl.pallas_call(
        _in_proj_kernel,
        out_shape=(jax.ShapeDtypeStruct((m, N_MAIN), BF16),
                   jax.ShapeDtypeStruct((m, 128), F32)),
        grid=(m // tm, N_MAIN // IN_PROJ_WIDE_COLS),
        in_specs=[x_piece(p) for p in range(X_PIECES)] + [
            pl.BlockSpec((1, d), lambda i, j: (0, 0)),
            pl.BlockSpec((None, d, IN_PROJ_WIDE_COLS), lambda i, j: (layer, 0, j)),
            pl.BlockSpec((None, d, 128), lambda i, j: (layer, 0, 0)),
            pl.BlockSpec((1, 1, IN_PROJ_WIDE_COLS), lambda i, j: (j, 0, 0)),
        ],
        out_specs=(pl.BlockSpec((tm, IN_PROJ_WIDE_COLS), lambda i, j: (i, j)),
                   pl.BlockSpec((tm, 128), lambda i, j: (i, 0))),
        scratch_shapes=[pltpu.VMEM((tm, d), BF16)],
        compiler_params=pltpu.CompilerParams(
            dimension_semantics=("arbitrary", "arbitrary"),
            vmem_limit_bytes=VMEM_LIMIT_BYTES),
        name="in_proj",
    )(*([x2] * X_PIECES), g, w_main, w_lr, gains)


def _run_phase_major(chains, n_phases):
    for _ in range(n_phases):
        for chain in chains:
            next(chain)


def _gla_constants():
    c = GLA_CHUNK
    idx = np.arange(c)
    tri = np.zeros((2, 2 * len(GLA_LEVELS) * c, c), np.float32)
    msk = np.zeros((2, len(GLA_LEVELS), c, c), np.float32)
    for d in range(2):
        covered = np.zeros((c, c), bool)
        for n, l in enumerate(GLA_LEVELS):
            blk = idx // l
            same = blk[:, None] == blk[None, :]
            if d == 0:
                wq = same & (idx[None, :] <= idx[:, None])
                wk = same & (idx[None, :] > idx[:, None])
                adj = blk[:, None] == blk[None, :] + 1
            else:
                wq = same & (idx[None, :] >= idx[:, None])
                wk = same & (idx[None, :] < idx[:, None])
                adj = blk[None, :] == blk[:, None] + 1
            tri[d, n * c:(n + 1) * c] = wq
            tri[d, (len(GLA_LEVELS) + n) * c:(len(GLA_LEVELS) + n + 1) * c] = wk
            if l != c:
                m = adj & ~covered
                covered |= m
                msk[d, n] = m
    tri = np.concatenate([tri, tri], axis=2)
    return jnp.asarray(tri, BF16), jnp.asarray(msk[:, :len(GLA_LEVELS) - 1], F32)


def _log_sigmoid(z):
    return jnp.minimum(z, 0.0) - jnp.log(1.0 + jnp.exp(-jnp.abs(z)))


def _gla_chunk(d, head, r0, q_ref, k_ref, v_ref, g_scr, st_scr, o_ref, tri_ref, msk_ref,
               diag_masks):
    c, nl = GLA_CHUNK, len(GLA_LEVELS)
    ks = slice(head * GLA_DK, (head + 1) * GLA_DK)
    vs = slice(head * GLA_DV, (head + 1) * GLA_DV)
    rows = pl.ds(r0, c)
    q = q_ref[0, rows, ks].astype(F32)
    k = k_ref[0, rows, ks].astype(F32)
    v = v_ref[0, rows, vs]
    g = g_scr[rows, ks]

    g_hi = g.astype(BF16)
    g_lo = (g - g_hi.astype(F32)).astype(BF16)
    sums = _dot(tri_ref[d], jnp.concatenate([g_hi, g_lo], axis=0))
    yield
    e = jnp.exp2(sums)

    def lvl(n):
        eq = e[n * c:(n + 1) * c]
        ek = e[(nl + n) * c:(nl + n + 1) * c]
        return (q * eq).astype(BF16), (k * ek).astype(BF16)

    lvl_dots = []
    for n in range(nl - 1):
        qn, kn = lvl(n)
        lvl_dots.append(_dot_nt(qn, kn))
    q64, k64 = lvl(nl - 1)
    st = st_scr[head]
    o_inter = _dot_nt(q64, st.astype(BF16))
    st_update = _dot_tn(v, k64)
    yield

    blocks = lambda x: x.reshape(c // 8, 8, GLA_DK)
    step = 1 if d == 0 else 7
    q3, kd, decay = blocks(q), blocks(k), blocks(jnp.exp2(g))
    a = jnp.where(diag_masks[d][0], jnp.sum(q * k, axis=-1, keepdims=True), 0.0)
    for lag in range(1, 8):
        kd = decay * pltpu.roll(kd, step, 1)
        col = jnp.sum(q3 * kd, axis=-1, keepdims=True).reshape(c, 1)
        a = jnp.where(diag_masks[d][lag], col, a)
    yield

    for n in range(nl - 1):
        a = a + msk_ref[d, n] * lvl_dots[n]
    o = _dot(a.astype(BF16), v) + o_inter
    o_ref[0, rows, vs] = o.astype(o_ref.dtype)

    tot_row = (nl - 1) * c + (c - 1 if d == 0 else 0)
    st_scr[head] = st * e[tot_row:tot_row + 1] + st_update
    yield


def _gla_kernel(qf_ref, kf_ref, vf_ref, lrf_ref, qb_ref, kb_ref, vb_ref, lrb_ref,
                w2f_ref, bf_ref, w2b_ref, bb_ref, tri_ref, msk_ref,
                of_ref, ob_ref, stf_scr, stb_scr, gf_scr, gb_scr):
    t = pl.program_id(1)

    @pl.when(t == 0)
    def _():
        stf_scr[...] = jnp.zeros_like(stf_scr)
        stb_scr[...] = jnp.zeros_like(stb_scr)

    to_log2_decay = LOG2_E / GLA_TAU
    zf = _dot(lrf_ref[0].astype(BF16), w2f_ref[...]) + bf_ref[...]
    gf_scr[...] = _log_sigmoid(zf) * to_log2_decay
    zb = _dot(lrb_ref[0].astype(BF16), w2b_ref[...]) + bb_ref[...]
    gb_scr[...] = _log_sigmoid(zb) * to_log2_decay

    n_chunks = qf_ref.shape[1] // GLA_CHUNK

    row = lax.broadcasted_iota(jnp.int32, (GLA_CHUNK, GLA_CHUNK), 0)
    col = lax.broadcasted_iota(jnp.int32, (GLA_CHUNK, GLA_CHUNK), 1)
    same_block = (row // 8) == (col // 8)
    diag_masks = [[same_block & (col == row - lag) for lag in range(8)],
                  [same_block & (col == row + lag) for lag in range(8)]]

    def body(ci, carry):
        rf = pl.multiple_of(ci * GLA_CHUNK, GLA_CHUNK)
        rb = pl.multiple_of((n_chunks - 1 - ci) * GLA_CHUNK, GLA_CHUNK)
        streams = []
        for head in range(GLA_HEADS):
            streams.append(_gla_chunk(0, head, rf, qf_ref, kf_ref, vf_ref, gf_scr, stf_scr,
                                      of_ref, tri_ref, msk_ref, diag_masks))
            streams.append(_gla_chunk(1, head, rb, qb_ref, kb_ref, vb_ref, gb_scr, stb_scr,
                                      ob_ref, tri_ref, msk_ref, diag_masks))
        _run_phase_major(streams, GLA_PHASES)
        return carry

    lax.fori_loop(0, n_chunks, body, 0, unroll=GLA_CHUNK_UNROLL)


def _gla(proj3, glr3, w2f, bf, w2b, bb, tri, msk, tb):
    b, s, _ = proj3.shape
    nt = s // tb
    qk_w, v_w = GLA_HEADS * GLA_DK, GLA_HEADS * GLA_DV
    fwd = lambda col: (lambda bi, t: (bi, t, col))
    bwd = lambda col: (lambda bi, t: (bi, nt - 1 - t, col))
    const2 = lambda bi, t: (0, 0)
    in_specs = []
    for mk in (fwd, bwd):
        in_specs += [
            pl.BlockSpec((1, tb, qk_w), mk(0)),
            pl.BlockSpec((1, tb, qk_w), mk(1)),
            pl.BlockSpec((1, tb, v_w), mk(1)),
            pl.BlockSpec((1, tb, 128), mk(0)),
        ]
    in_specs += [
        pl.BlockSpec((128, qk_w), const2), pl.BlockSpec((1, qk_w), const2),
        pl.BlockSpec((128, qk_w), const2), pl.BlockSpec((1, qk_w), const2),
        pl.BlockSpec(tri.shape, lambda bi, t: (0, 0, 0)),
        pl.BlockSpec(msk.shape, lambda bi, t: (0, 0, 0, 0)),
    ]
    return pl.pallas_call(
        _gla_kernel,
        out_shape=(jax.ShapeDtypeStruct((b, s, v_w), BF16),
                   jax.ShapeDtypeStruct((b, s, v_w), BF16)),
        grid=(b, nt),
        in_specs=in_specs,
        out_specs=(pl.BlockSpec((1, tb, v_w), fwd(0)),
                   pl.BlockSpec((1, tb, v_w), bwd(0))),
        scratch_shapes=[pltpu.VMEM((GLA_HEADS, GLA_DV, GLA_DK), F32),
                        pltpu.VMEM((GLA_HEADS, GLA_DV, GLA_DK), F32),
                        pltpu.VMEM((tb, qk_w), F32),
                        pltpu.VMEM((tb, qk_w), F32)],
        compiler_params=pltpu.CompilerParams(
            dimension_semantics=("arbitrary", "arbitrary"),
            vmem_limit_bytes=VMEM_LIMIT_BYTES),
        name="gla",
    )(proj3, proj3, proj3, glr3, proj3, proj3, proj3, glr3, w2f, bf, w2b, bb, tri, msk)


def _na_bias_table(rpb):
    col = np.arange(GRID_W)
    cs = np.clip(col - NA_COLS // 2, 0, GRID_W - NA_COLS)
    inside = (col[None, :] >= cs[:, None]) & (col[None, :] < cs[:, None] + NA_COLS)
    dc = col[None, :] - col[:, None] + (NA_COLS - 1)
    onehot = (np.arange(2 * NA_COLS - 1)[:, None, None] == dc[None]) & inside[None]
    t = jnp.einsum("lhdc,cqk->lhdqk", rpb.astype(F32), jnp.asarray(onehot, F32),
                   precision=lax.Precision.HIGHEST)
    t = jnp.where(inside[None, None, None], t, NEG_BIG)
    return jnp.concatenate([t[:, :, :-1], t[:, :, 1:]], axis=-1)


def _na_kernel(q_ref, k_ref, v_ref, t2_ref, o_ref, *, n_rows):
    rb = q_ref.shape[1] // GRID_W
    t = pl.program_id(1)
    gw = NA_GROUP * NA_HD
    head_of_lane = lax.broadcasted_iota(jnp.int32, (GRID_W, gw), 1) // NA_HD
    n_keys = NA_ROWS * GRID_W

    def pick(blocks):
        out = blocks[(NA_GROUP - 1) * GRID_W:]
        for h in range(NA_GROUP - 2, -1, -1):
            out = jnp.where(head_of_lane == h, blocks[h * GRID_W:(h + 1) * GRID_W], out)
        return out

    def body(rr, carry):
        r = t * rb + rr
        rs = jnp.clip(r - NA_ROWS // 2, 0, n_rows - NA_ROWS)
        off = rs - r + (NA_ROWS - 1)
        qrows = pl.ds(pl.multiple_of(rr * GRID_W, GRID_W), GRID_W)
        krows = pl.ds(pl.multiple_of(rs * GRID_W, GRID_W), n_keys)
        for grp in range(NA_HEADS // NA_GROUP):
            ls = slice(grp * gw, (grp + 1) * gw)
            q4 = q_ref[0, qrows, ls]
            k4 = k_ref[0, krows, ls]
            v4 = v_ref[0, krows, ls]
            lhs = jnp.concatenate(
                [jnp.where(head_of_lane == h, q4, jnp.zeros_like(q4)) for h in range(NA_GROUP)],
                axis=0)
            bias = jnp.concatenate(
                [jnp.concatenate([t2_ref[grp * NA_GROUP + h, off + 2 * w]
                                  for w in range(NA_ROWS // 2)], axis=1)
                 for h in range(NA_GROUP)], axis=0)
            s = _dot_nt(lhs, k4) + bias
            m = jnp.max(s, axis=-1, keepdims=True)
            pe = jnp.exp(s - m)
            den = jnp.sum(pe, axis=-1, keepdims=True)
            pv = _dot(pe.astype(BF16), v4)
            o_ref[0, qrows, ls] = (pick(pv) / pick(den)).astype(o_ref.dtype)
        return carry

    lax.fori_loop(0, rb, body, 0, unroll=True)


def _na(proj3, t2, layer, rb):
    b, s, _ = proj3.shape
    n_rows = s // GRID_W
    w = NA_HEADS * NA_HD
    return pl.pallas_call(
        functools.partial(_na_kernel, n_rows=n_rows),
        out_shape=jax.ShapeDtypeStruct((b, s, w), BF16),
        grid=(b, n_rows // rb),
        in_specs=[
            pl.BlockSpec((1, rb * GRID_W, w), lambda bi, t: (bi, t, T_NQ)),
            pl.BlockSpec((1, s, w), lambda bi, t: (bi, 0, T_NK)),
            pl.BlockSpec((1, s, w), lambda bi, t: (bi, 0, T_NV)),
            pl.BlockSpec((None,) + t2.shape[1:], lambda bi, t: (layer, 0, 0, 0, 0)),
        ],
        out_specs=pl.BlockSpec((1, rb * GRID_W, w), lambda bi, t: (bi, t, 0)),
        compiler_params=pltpu.CompilerParams(
            dimension_semantics=("arbitrary", "arbitrary"),
            vmem_limit_bytes=VMEM_LIMIT_BYTES),
        name="na",
    )(proj3, proj3, proj3, t2)


def _mem_kv_kernel(mem_ref, g_ref, w_ref, kg_ref, k_ref, v_ref):
    xf = mem_ref[0]
    ms = jnp.mean(xf * xf, axis=-1, keepdims=True)
    h = (xf * lax.rsqrt(ms + EPS) * g_ref[...]).astype(BF16)
    kv = _dot(h, w_ref[...])
    wk = MEM_HEADS * MEM_HD
    parts = []
    for hd in range(MEM_HEADS):
        kh = kv[:, hd * MEM_HD:(hd + 1) * MEM_HD]
        msk = jnp.mean(kh * kh, axis=-1, keepdims=True)
        parts.append(kh * lax.rsqrt(msk + EPS) * kg_ref[...])
    k_ref[0] = jnp.concatenate(parts, axis=1).astype(BF16)
    v_ref[0] = kv[:, wk:].astype(BF16)


def _mem_kv(mem, g, w, kg, layer):
    b, m, d = mem.shape
    wk = MEM_HEADS * MEM_HD
    return pl.pallas_call(
        _mem_kv_kernel,
        out_shape=(jax.ShapeDtypeStruct((b, m, wk), BF16),
                   jax.ShapeDtypeStruct((b, m, wk), BF16)),
        grid=(b,),
        in_specs=[
            pl.BlockSpec((1, m, d), lambda bi: (bi, 0, 0)),
            pl.BlockSpec((1, d), lambda bi: (0, 0)),
            pl.BlockSpec((None, d, 2 * wk), lambda bi: (layer, 0, 0)),
            pl.BlockSpec((1, MEM_HD), lambda bi: (0, 0)),
        ],
        out_specs=(pl.BlockSpec((1, m, wk), lambda bi: (bi, 0, 0)),
                   pl.BlockSpec((1, m, wk), lambda bi: (bi, 0, 0))),
        compiler_params=pltpu.CompilerParams(
            dimension_semantics=("arbitrary",),
            vmem_limit_bytes=VMEM_LIMIT_BYTES),
        name="mem_kv",
    )(mem, g, w, kg)


def _merge_kernel(of_ref, ob_ref, ggate_ref, nb_ref, ngate_ref, mq_ref, mgate_ref, sig_ref,
                  x_ref, kc_ref, vc_ref, gout_ref, pa_ref, pb_ref, pc_ref, wo_ref, out_ref):
    d = x_ref.shape[1]
    o = of_ref[...].astype(F32) + ob_ref[...].astype(F32)
    parts = []
    for h in range(GLA_HEADS):
        oh = o[:, h * GLA_DV:(h + 1) * GLA_DV]
        ms = jnp.mean(oh * oh, axis=-1, keepdims=True)
        parts.append(oh * lax.rsqrt(ms + EPS) * gout_ref[...])
    oa = jnp.concatenate(parts, axis=1) * ggate_ref[...].astype(F32)
    ya = _dot(oa.astype(BF16), pa_ref[...])

    nb = nb_ref[...].astype(F32) * ngate_ref[...].astype(F32)
    yb = _dot(nb.astype(BF16), pb_ref[...])

    parts = []
    for h in range(MEM_HEADS):
        hs = slice(h * MEM_HD, (h + 1) * MEM_HD)
        s = _dot_nt(mq_ref[:, hs], kc_ref[0, :, hs])
        m = jnp.max(s, axis=-1, keepdims=True)
        pe = jnp.exp(s - m)
        den = jnp.sum(pe, axis=-1, keepdims=True)
        parts.append(_dot(pe.astype(BF16), vc_ref[0, :, hs]) / den)
    oc = jnp.concatenate(parts, axis=1) * mgate_ref[...].astype(F32)
    yc = _dot(oc.astype(BF16), pc_ref[...])

    y = (sig_ref[:, 0:d].astype(F32) * ya + sig_ref[:, d:2 * d].astype(F32) * yb
         + sig_ref[:, 2 * d:3 * d].astype(F32) * yc)
    out_ref[...] = x_ref[...] + _dot(y.astype(BF16), wo_ref[...])


def _merge(of2, ob2, nb2, proj2, x2, kc, vc, gout, pa, pb, pc, wo, layer, tm, seq):
    m, d = x2.shape
    v_w = GLA_HEADS * GLA_DV
    na_w = NA_HEADS * NA_HD
    mem_w = MEM_HEADS * MEM_HD
    steps_per_batch = seq // tm
    row = lambda col: (lambda i: (i, col))
    const = lambda i: (0, 0)
    weight = lambda w: pl.BlockSpec((None,) + w.shape[1:], lambda i: (layer, 0, 0))
    batch = lambda i: (i // steps_per_batch, 0, 0)
    return pl.pallas_call(
        _merge_kernel,
        out_shape=jax.ShapeDtypeStruct((m, d), F32),
        grid=(m // tm,),
        in_specs=[
            pl.BlockSpec((tm, v_w), row(0)),
            pl.BlockSpec((tm, v_w), row(0)),
            pl.BlockSpec((tm, v_w), row(T_GGATE * COL_TILE // v_w)),
            pl.BlockSpec((tm, na_w), row(0)),
            pl.BlockSpec((tm, na_w), row(T_NGATE)),
            pl.BlockSpec((tm, mem_w), row(T_MQ)),
            pl.BlockSpec((tm, mem_w), row(T_MGATE)),
            pl.BlockSpec((tm, 3 * d), row(T_MERGE * COL_TILE // (3 * d))),
            pl.BlockSpec((tm, d), row(0)),
            pl.BlockSpec((1,) + kc.shape[1:], batch),
            pl.BlockSpec((1,) + vc.shape[1:], batch),
            pl.BlockSpec((1, GLA_DV), const),
            weight(pa), weight(pb), weight(pc), weight(wo),
        ],
        out_specs=pl.BlockSpec((tm, d), row(0)),
        compiler_params=pltpu.CompilerParams(
            dimension_semantics=("arbitrary",),
            vmem_limit_bytes=VMEM_LIMIT_BYTES),
        name="merge",
    )(of2, ob2, proj2, nb2, proj2, proj2, proj2, proj2, x2, kc, vc, gout, pa, pb, pc, wo)


def _stacked_weights(w_in, na_rpb, w_mem_kv, p_a, p_b, p_c, w_out):
    qk_w = GLA_HEADS * GLA_DK
    lr0 = 2 * qk_w + 2 * GLA_HEADS * GLA_DV
    lr1 = lr0 + 2 * GLA_RANK
    w_main = jnp.concatenate([w_in[:, :, :lr0], w_in[:, :, lr1:]], axis=2).astype(BF16)
    w_lr = jnp.pad(w_in[:, :, lr0:lr1], ((0, 0), (0, 0), (0, 128 - 2 * GLA_RANK))).astype(BF16)
    return dict(w_main=w_main, w_lr=w_lr, t2=_na_bias_table(na_rpb),
                w_kv=w_mem_kv.astype(BF16), pa=p_a.astype(BF16), pb=p_b.astype(BF16),
                pc=p_c.astype(BF16), wo=w_out.astype(BF16))


def _layer_params(l, norm_g, gla_w2_f, gla_b_f, gla_w2_b, gla_b_b, gla_out_g, na_q_g, na_k_g,
                  mem_norm_g, mem_q_g, mem_k_g):
    d = norm_g.shape[1]
    qk_w = GLA_HEADS * GLA_DK
    gains = jnp.ones((N_COL_TILES, COL_TILE), F32)
    gains = gains.at[T_GQ].set(GLA_DK ** -0.5)
    gains = gains.at[T_NQ].set(jnp.tile(na_q_g[l].astype(F32) * NA_HD ** -0.5, NA_HEADS))
    gains = gains.at[T_NK].set(jnp.tile(na_k_g[l].astype(F32), NA_HEADS))
    gains = gains.at[T_MQ].set(jnp.tile(mem_q_g[l].astype(F32) * MEM_HD ** -0.5, MEM_HEADS))

    w2f = jnp.zeros((128, qk_w), F32).at[:GLA_RANK].set(gla_w2_f[l]).astype(BF16)
    w2b = jnp.zeros((128, qk_w), F32).at[GLA_RANK:2 * GLA_RANK].set(gla_w2_b[l]).astype(BF16)
    return dict(
        norm_g=norm_g[l].reshape(1, d).astype(F32),
        gains=gains.reshape(N_MAIN // IN_PROJ_WIDE_COLS, 1, IN_PROJ_WIDE_COLS),
        w2f=w2f, bf=gla_b_f[l].reshape(1, qk_w).astype(F32),
        w2b=w2b, bb=gla_b_b[l].reshape(1, qk_w).astype(F32),
        gout=gla_out_g[l].reshape(1, GLA_DV).astype(F32),
        mem_g=mem_norm_g[l].reshape(1, d).astype(F32),
        mem_kg=mem_k_g[l].reshape(1, MEM_HD).astype(F32),
    )


def _tiles(batch, seq):
    m = batch * seq
    tm_in = 2048 if m % 2048 == 0 else seq
    tm_merge = 512 if seq % 512 == 0 else seq
    tb_gla = 512 if seq % 512 == 0 else seq
    rb_na = 8
    return tm_in, tm_merge, tb_gla, rb_na


def kernel(x, mem, norm_g, w_in, gla_w2_f, gla_b_f, gla_w2_b, gla_b_b, gla_out_g, p_a, na_q_g,
           na_k_g, na_rpb, p_b, mem_norm_g, w_mem_kv, mem_q_g, mem_k_g, p_c, w_out):
    b, s, d = x.shape
    depth = w_in.shape[0]
    n_rows = s // GRID_W
    assert s % GRID_W == 0 and n_rows >= NA_ROWS and s % GLA_CHUNK == 0
    assert w_in.shape[2] - 2 * GLA_RANK == N_MAIN
    tm_in, tm_merge, tb_gla, rb_na = _tiles(b, s)
    tri, msk = _gla_constants()

    big = _stacked_weights(w_in, na_rpb, w_mem_kv, p_a, p_b, p_c, w_out)
    x2 = x.reshape(b * s, d).astype(F32)
    for l in range(depth):
        p = _layer_params(l, norm_g, gla_w2_f, gla_b_f, gla_w2_b, gla_b_b, gla_out_g, na_q_g,
                          na_k_g, mem_norm_g, mem_q_g, mem_k_g)
        proj2, glr2 = _in_proj(x2, p["norm_g"], big["w_main"], big["w_lr"], p["gains"], l, tm_in)
        proj3 = proj2.reshape(b, s, N_MAIN)
        glr3 = glr2.reshape(b, s, 128)
        o_f, o_b = _gla(proj3, glr3, p["w2f"], p["bf"], p["w2b"], p["bb"], tri, msk, tb_gla)
        nb = _na(proj3, big["t2"], l, rb_na)
        kc, vc = _mem_kv(mem.astype(F32), p["mem_g"], big["w_kv"], p["mem_kg"], l)
        x2 = _merge(o_f.reshape(b * s, -1), o_b.reshape(b * s, -1), nb.reshape(b * s, -1),
                    proj2, x2, kc, vc, p["gout"], big["pa"], big["pb"], big["pc"], big["wo"],
                    l, tm_merge, s)
    return x2.reshape(b, s, d).astype(x.dtype)
```

```python
import functools

import numpy as np
import jax
import jax.numpy as jnp
from jax import lax
from jax.experimental import pallas as pl
from jax.experimental.pallas import tpu as pltpu

F32 = jnp.float32
BF16 = jnp.bfloat16

EPS = 1e-6
NEG_BIG = -1e30
LOG2_E = 1.4426950408889634

GRID_W = 64
GLA_HEADS, GLA_DK, GLA_DV = 4, 128, 256
GLA_RANK = 16
GLA_TAU = 16.0
GLA_CHUNK = 64
GLA_LEVELS = (8, 16, 32, 64)
GLA_PHASES = 4
GLA_CHUNK_UNROLL = 4
NA_HEADS, NA_HD = 8, 64
NA_ROWS, NA_COLS = 8, 16
NA_GROUP = 4
MEM_HEADS, MEM_HD = 4, 128

COL_TILE = 512
N_MAIN = 9216
T_GQ, T_GK, T_GV, T_GGATE = 0, 1, 2, 4
T_NQ, T_NK, T_NV, T_NGATE = 6, 7, 8, 9
T_MQ, T_MGATE, T_MERGE = 10, 11, 12
N_COL_TILES = N_MAIN // COL_TILE
IN_PROJ_WIDE_COLS = 3 * COL_TILE
X_PIECES = 4
IN_PROJ_SUB_ROWS = 256

VMEM_LIMIT_BYTES = 48 * 1024 * 1024


def _dot(a, b):
    return jnp.dot(a, b, preferred_element_type=F32)


def _dot_nt(a, b):
    return lax.dot_general(a, b, (((1,), (1,)), ((), ())), preferred_element_type=F32)


def _dot_tn(a, b):
    return lax.dot_general(a, b, (((0,), (0,)), ((), ())), preferred_element_type=F32)


def _col_kinds():
    kinds = ["lin"] * N_COL_TILES
    for t in (T_GGATE, T_GGATE + 1, T_NGATE, T_MGATE):
        kinds[t] = "silu"
    kinds[T_NQ] = kinds[T_NK] = "n64"
    kinds[T_MQ] = "n128"
    kinds[T_MERGE:] = ["sig"] * (N_COL_TILES - T_MERGE)
    return kinds


def _in_proj_kernel(*refs):
    x_refs = refs[:X_PIECES]
    g_ref, w_ref, wlr_ref, gain_ref, proj_ref, glr_ref, h_ref = refs[X_PIECES:]
    j = pl.program_id(1)

    @pl.when(j == 0)
    def _():
        xf = jnp.concatenate([x_ref[...] for x_ref in x_refs], axis=1)
        ms = jnp.mean(xf * xf, axis=-1, keepdims=True)
        h = (xf * lax.rsqrt(ms + EPS) * g_ref[...]).astype(BF16)
        h_ref[...] = h
        glr_ref[...] = _dot(h, wlr_ref[...])

    gain = gain_ref[0]

    def normed(head_w):
        def epilogue(acc, gain_cols):
            sq = acc * acc
            lane = lax.broadcasted_iota(jnp.int32, (acc.shape[0], 128), 1)
            groups = []
            for c0 in range(0, COL_TILE, 128):
                sq_g = sq[:, c0:c0 + 128]
                scale = None
                for h0 in range(0, 128, head_w):
                    mine = (lane >= h0) & (lane < h0 + head_w)
                    ms = jnp.sum(jnp.where(mine, sq_g, 0.0), axis=-1, keepdims=True) * (1.0 / head_w)
                    r = lax.rsqrt(ms + EPS)
                    scale = r if scale is None else jnp.where(mine, r, scale)
                groups.append(jnp.broadcast_to(scale, sq_g.shape))
            return acc * jnp.concatenate(groups, axis=1) * gain_cols
        return epilogue

    epilogues = {
        "lin": lambda acc, gain_cols: acc * gain_cols,
        "silu": lambda acc, gain_cols: acc * jax.nn.sigmoid(acc),
        "sig": lambda acc, gain_cols: jax.nn.sigmoid(acc),
        "n64": normed(NA_HD),
        "n128": normed(MEM_HD),
    }

    def tile(kinds):
        for r0 in range(0, h_ref.shape[0], IN_PROJ_SUB_ROWS):
            rows = slice(r0, r0 + IN_PROJ_SUB_ROWS)
            acc = _dot(h_ref[rows, :], w_ref[...])
            for n, kind in enumerate(kinds):
                cols = slice(n * COL_TILE, (n + 1) * COL_TILE)
                proj_ref[rows, cols] = epilogues[kind](acc[:, cols], gain[:, cols]).astype(BF16)

    col_kinds = _col_kinds()
    wide = IN_PROJ_WIDE_COLS // COL_TILE
    combos = {}
    for jt in range(N_COL_TILES // wide):
        combos.setdefault(tuple(col_kinds[jt * wide:(jt + 1) * wide]), []).append(jt)
    for kinds, tiles in combos.items():
        cond = j == tiles[0]
        for jt in tiles[1:]:
            cond = cond | (j == jt)
        pl.when(cond)(functools.partial(tile, kinds))


def _in_proj(x2, g, w_main, w_lr, gains, layer, tm):
    m, d = x2.shape
    n_row_tiles = m // tm

    def x_piece(p):
        switch = N_MAIN // IN_PROJ_WIDE_COLS - X_PIECES + p
        return pl.BlockSpec(
            (tm, d // X_PIECES),
            lambda i, j: (jnp.minimum(i + (j > switch).astype(jnp.int32), n_row_tiles - 1), p))

    return pl.pallas_call(
        _in_proj_kernel,
        out_shape=(jax.ShapeDtypeStruct((m, N_MAIN), BF16),
                   jax.ShapeDtypeStruct((m, 128), F32)),
        grid=(m // tm, N_MAIN // IN_PROJ_WIDE_COLS),
        in_specs=[x_piece(p) for p in range(X_PIECES)] + [
            pl.BlockSpec((1, d), lambda i, j: (0, 0)),
            pl.BlockSpec((None, d, IN_PROJ_WIDE_COLS), lambda i, j: (layer, 0, j)),
            pl.BlockSpec((None, d, 128), lambda i, j: (layer, 0, 0)),
            pl.BlockSpec((1, 1, IN_PROJ_WIDE_COLS), lambda i, j: (j, 0, 0)),
        ],
        out_specs=(pl.BlockSpec((tm, IN_PROJ_WIDE_COLS), lambda i, j: (i, j)),
                   pl.BlockSpec((tm, 128), lambda i, j: (i, 0))),
        scratch_shapes=[pltpu.VMEM((tm, d), BF16)],
        compiler_params=pltpu.CompilerParams(
            dimension_semantics=("arbitrary", "arbitrary"),
            vmem_limit_bytes=VMEM_LIMIT_BYTES),
        name="in_proj",
    )(*([x2] * X_PIECES), g, w_main, w_lr, gains)


def _run_phase_major(chains, n_phases):
    for _ in range(n_phases):
        for chain in chains:
            next(chain)


def _gla_constants():
    c = GLA_CHUNK
    idx = np.arange(c)
    tri = np.zeros((2, 2 * len(GLA_LEVELS) * c, c), np.float32)
    msk = np.zeros((2, len(GLA_LEVELS), c, c), np.float32)
    for d in range(2):
        covered = np.zeros((c, c), bool)
        for n, l in enumerate(GLA_LEVELS):
            blk = idx // l
            same = blk[:, None] == blk[None, :]
            if d == 0:
                wq = same & (idx[None, :] <= idx[:, None])
                wk = same & (idx[None, :] > idx[:, None])
                adj = blk[:, None] == blk[None, :] + 1
            else:
                wq = same & (idx[None, :] >= idx[:, None])
                wk = same & (idx[None, :] < idx[:, None])
                adj = blk[None, :] == blk[:, None] + 1
            tri[d, n * c:(n + 1) * c] = wq
            tri[d, (len(GLA_LEVELS) + n) * c:(len(GLA_LEVELS) + n + 1) * c] = wk
            if l != c:
                m = adj & ~covered
                covered |= m
                msk[d, n] = m
    tri = np.concatenate([tri, tri], axis=2)
    return jnp.asarray(tri, BF16), jnp.asarray(msk[:, :len(GLA_LEVELS) - 1], F32)


def _log_sigmoid(z):
    return jnp.minimum(z, 0.0) - jnp.log(1.0 + jnp.exp(-jnp.abs(z)))


def _gla_chunk(d, head, r0, q_ref, k_ref, v_ref, g_scr, st_scr, o_ref, tri_ref, msk_ref,
               diag_masks):
    c, nl = GLA_CHUNK, len(GLA_LEVELS)
    ks = slice(head * GLA_DK, (head + 1) * GLA_DK)
    vs = slice(head * GLA_DV, (head + 1) * GLA_DV)
    rows = pl.ds(r0, c)
    q = q_ref[0, rows, ks].astype(F32)
    k = k_ref[0, rows, ks].astype(F32)
    v = v_ref[0, rows, vs]
    g = g_scr[rows, ks]

    g_hi = g.astype(BF16)
    g_lo = (g - g_hi.astype(F32)).astype(BF16)
    sums = _dot(tri_ref[d], jnp.concatenate([g_hi, g_lo], axis=0))
    yield
    e = jnp.exp2(sums)

    def lvl(n):
        eq = e[n * c:(n + 1) * c]
        ek = e[(nl + n) * c:(nl + n + 1) * c]
        return (q * eq).astype(BF16), (k * ek).astype(BF16)

    lvl_dots = []
    for n in range(nl - 1):
        qn, kn = lvl(n)
        lvl_dots.append(_dot_nt(qn, kn))
    q64, k64 = lvl(nl - 1)
    st = st_scr[head]
    o_inter = _dot_nt(q64, st.astype(BF16))
    st_update = _dot_tn(v, k64)
    yield

    blocks = lambda x: x.reshape(c // 8, 8, GLA_DK)
    step = 1 if d == 0 else 7
    q3, kd, decay = blocks(q), blocks(k), blocks(jnp.exp2(g))
    a = jnp.where(diag_masks[d][0], jnp.sum(q * k, axis=-1, keepdims=True), 0.0)
    for lag in range(1, 8):
        kd = decay * pltpu.roll(kd, step, 1)
        col = jnp.sum(q3 * kd, axis=-1, keepdims=True).reshape(c, 1)
        a = jnp.where(diag_masks[d][lag], col, a)
    yield

    for n in range(nl - 1):
        a = a + msk_ref[d, n] * lvl_dots[n]
    o = _dot(a.astype(BF16), v) + o_inter
    o_ref[0, rows, vs] = o.astype(o_ref.dtype)

    tot_row = (nl - 1) * c + (c - 1 if d == 0 else 0)
    st_scr[head] = st * e[tot_row:tot_row + 1] + st_update
    yield


def _gla_kernel(qf_ref, kf_ref, vf_ref, lrf_ref, qb_ref, kb_ref, vb_ref, lrb_ref,
                w2f_ref, bf_ref, w2b_ref, bb_ref, tri_ref, msk_ref,
                of_ref, ob_ref, stf_scr, stb_scr, gf_scr, gb_scr):
    t = pl.program_id(1)

    @pl.when(t == 0)
    def _():
        stf_scr[...] = jnp.zeros_like(stf_scr)
        stb_scr[...] = jnp.zeros_like(stb_scr)

    to_log2_decay = LOG2_E / GLA_TAU
    zf = _dot(lrf_ref[0].astype(BF16), w2f_ref[...]) + bf_ref[...]
    gf_scr[...] = _log_sigmoid(zf) * to_log2_decay
    zb = _dot(lrb_ref[0].astype(BF16), w2b_ref[...]) + bb_ref[...]
    gb_scr[...] = _log_sigmoid(zb) * to_log2_decay

    n_chunks = qf_ref.shape[1] // GLA_CHUNK

    row = lax.broadcasted_iota(jnp.int32, (GLA_CHUNK, GLA_CHUNK), 0)
    col = lax.broadcasted_iota(jnp.int32, (GLA_CHUNK, GLA_CHUNK), 1)
    same_block = (row // 8) == (col // 8)
    diag_masks = [[same_block & (col == row - lag) for lag in range(8)],
                  [same_block & (col == row + lag) for lag in range(8)]]

    def body(ci, carry):
        rf = pl.multiple_of(ci * GLA_CHUNK, GLA_CHUNK)
        rb = pl.multiple_of((n_chunks - 1 - ci) * GLA_CHUNK, GLA_CHUNK)
        streams = []
        for head in range(GLA_HEADS):
            streams.append(_gla_chunk(0, head, rf, qf_ref, kf_ref, vf_ref, gf_scr, stf_scr,
                                      of_ref, tri_ref, msk_ref, diag_masks))
            streams.append(_gla_chunk(1, head, rb, qb_ref, kb_ref, vb_ref, gb_scr, stb_scr,
                                      ob_ref, tri_ref, msk_ref, diag_masks))
        _run_phase_major(streams, GLA_PHASES)
        return carry

    lax.fori_loop(0, n_chunks, body, 0, unroll=GLA_CHUNK_UNROLL)


def _gla(proj3, glr3, w2f, bf, w2b, bb, tri, msk, tb):
    b, s, _ = proj3.shape
    nt = s // tb
    qk_w, v_w = GLA_HEADS * GLA_DK, GLA_HEADS * GLA_DV
    fwd = lambda col: (lambda bi, t: (bi, t, col))
    bwd = lambda col: (lambda bi, t: (bi, nt - 1 - t, col))
    const2 = lambda bi, t: (0, 0)
    in_specs = []
    for mk in (fwd, bwd):
        in_specs += [
            pl.BlockSpec((1, tb, qk_w), mk(0)),
            pl.BlockSpec((1, tb, qk_w), mk(1)),
            pl.BlockSpec((1, tb, v_w), mk(1)),
            pl.BlockSpec((1, tb, 128), mk(0)),
        ]
    in_specs += [
        pl.BlockSpec((128, qk_w), const2), pl.BlockSpec((1, qk_w), const2),
        pl.BlockSpec((128, qk_w), const2), pl.BlockSpec((1, qk_w), const2),
        pl.BlockSpec(tri.shape, lambda bi, t: (0, 0, 0)),
        pl.BlockSpec(msk.shape, lambda bi, t: (0, 0, 0, 0)),
    ]
    return pl.pallas_call(
        _gla_kernel,
        out_shape=(jax.ShapeDtypeStruct((b, s, v_w), BF16),
                   jax.ShapeDtypeStruct((b, s, v_w), BF16)),
        grid=(b, nt),
        in_specs=in_specs,
        out_specs=(pl.BlockSpec((1, tb, v_w), fwd(0)),
                   pl.BlockSpec((1, tb, v_w), bwd(0))),
        scratch_shapes=[pltpu.VMEM((GLA_HEADS, GLA_DV, GLA_DK), F32),
                        pltpu.VMEM((GLA_HEADS, GLA_DV, GLA_DK), F32),
                        pltpu.VMEM((tb, qk_w), F32),
                        pltpu.VMEM((tb, qk_w), F32)],
        compiler_params=pltpu.CompilerParams(
            dimension_semantics=("arbitrary", "arbitrary"),
            vmem_limit_bytes=VMEM_LIMIT_BYTES),
        name="gla",
    )(proj3, proj3, proj3, glr3, proj3, proj3, proj3, glr3, w2f, bf, w2b, bb, tri, msk)


def _na_bias_table(rpb):
    col = np.arange(GRID_W)
    cs = np.clip(col - NA_COLS // 2, 0, GRID_W - NA_COLS)
    inside = (col[None, :] >= cs[:, None]) & (col[None, :] < cs[:, None] + NA_COLS)
    dc = col[None, :] - col[:, None] + (NA_COLS - 1)
    onehot = (np.arange(2 * NA_COLS - 1)[:, None, None] == dc[None]) & inside[None]
    t = jnp.einsum("lhdc,cqk->lhdqk", rpb.astype(F32), jnp.asarray(onehot, F32),
                   precision=lax.Precision.HIGHEST)
    t = jnp.where(inside[None, None, None], t, NEG_BIG)
    return jnp.concatenate([t[:, :, :-1], t[:, :, 1:]], axis=-1)


def _na_kernel(q_ref, k_ref, v_ref, t2_ref, o_ref, *, n_rows):
    rb = q_ref.shape[1] // GRID_W
    t = pl.program_id(1)
    gw = NA_GROUP * NA_HD
    head_of_lane = lax.broadcasted_iota(jnp.int32, (GRID_W, gw), 1) // NA_HD
    n_keys = NA_ROWS * GRID_W

    def pick(blocks):
        out = blocks[(NA_GROUP - 1) * GRID_W:]
        for h in range(NA_GROUP - 2, -1, -1):
            out = jnp.where(head_of_lane == h, blocks[h * GRID_W:(h + 1) * GRID_W], out)
        return out

    def body(rr, carry):
        r = t * rb + rr
        rs = jnp.clip(r - NA_ROWS // 2, 0, n_rows - NA_ROWS)
        off = rs - r + (NA_ROWS - 1)
        qrows = pl.ds(pl.multiple_of(rr * GRID_W, GRID_W), GRID_W)
        krows = pl.ds(pl.multiple_of(rs * GRID_W, GRID_W), n_keys)
        for grp in range(NA_HEADS // NA_GROUP):
            ls = slice(grp * gw, (grp + 1) * gw)
            q4 = q_ref[0, qrows, ls]
            k4 = k_ref[0, krows, ls]
            v4 = v_ref[0, krows, ls]
            lhs = jnp.concatenate(
                [jnp.where(head_of_lane == h, q4, jnp.zeros_like(q4)) for h in range(NA_GROUP)],
                axis=0)
            bias = jnp.concatenate(
                [jnp.concatenate([t2_ref[grp * NA_GROUP + h, off + 2 * w]
                                  for w in range(NA_ROWS // 2)], axis=1)
                 for h in range(NA_GROUP)], axis=0)
            s = _dot_nt(lhs, k4) + bias
            m = jnp.max(s, axis=-1, keepdims=True)
            pe = jnp.exp(s - m)
            den = jnp.sum(pe, axis=-1, keepdims=True)
            pv = _dot(pe.astype(BF16), v4)
            o_ref[0, qrows, ls] = (pick(pv) / pick(den)).astype(o_ref.dtype)
        return carry

    lax.fori_loop(0, rb, body, 0, unroll=True)


def _na(proj3, t2, layer, rb):
    b, s, _ = proj3.shape
    n_rows = s // GRID_W
    w = NA_HEADS * NA_HD
    return pl.pallas_call(
        functools.partial(_na_kernel, n_rows=n_rows),
        out_shape=jax.ShapeDtypeStruct((b, s, w), BF16),
        grid=(b, n_rows // rb),
        in_specs=[
            pl.BlockSpec((1, rb * GRID_W, w), lambda bi, t: (bi, t, T_NQ)),
            pl.BlockSpec((1, s, w), lambda bi, t: (bi, 0, T_NK)),
            pl.BlockSpec((1, s, w), lambda bi, t: (bi, 0, T_NV)),
            pl.BlockSpec((None,) + t2.shape[1:], lambda bi, t: (layer, 0, 0, 0, 0)),
        ],
        out_specs=pl.BlockSpec((1, rb * GRID_W, w), lambda bi, t: (bi, t, 0)),
        compiler_params=pltpu.CompilerParams(
            dimension_semantics=("arbitrary", "arbitrary"),
            vmem_limit_bytes=VMEM_LIMIT_BYTES),
        name="na",
    )(proj3, proj3, proj3, t2)


def _mem_kv_kernel(mem_ref, g_ref, w_ref, kg_ref, k_ref, v_ref):
    xf = mem_ref[0]
    ms = jnp.mean(xf * xf, axis=-1, keepdims=True)
    h = (xf * lax.rsqrt(ms + EPS) * g_ref[...]).astype(BF16)
    kv = _dot(h, w_ref[...])
    wk = MEM_HEADS * MEM_HD
    parts = []
    for hd in range(MEM_HEADS):
        kh = kv[:, hd * MEM_HD:(hd + 1) * MEM_HD]
        msk = jnp.mean(kh * kh, axis=-1, keepdims=True)
        parts.append(kh * lax.rsqrt(msk + EPS) * kg_ref[...])
    k_ref[0] = jnp.concatenate(parts, axis=1).astype(BF16)
    v_ref[0] = kv[:, wk:].astype(BF16)


def _mem_kv(mem, g, w, kg, layer):
    b, m, d = mem.shape
    wk = MEM_HEADS * MEM_HD
    return pl.pallas_call(
        _mem_kv_kernel,
        out_shape=(jax.ShapeDtypeStruct((b, m, wk), BF16),
                   jax.ShapeDtypeStruct((b, m, wk), BF16)),
        grid=(b,),
        in_specs=[
            pl.BlockSpec((1, m, d), lambda bi: (bi, 0, 0)),
            pl.BlockSpec((1, d), lambda bi: (0, 0)),
            pl.BlockSpec((None, d, 2 * wk), lambda bi: (layer, 0, 0)),
            pl.BlockSpec((1, MEM_HD), lambda bi: (0, 0)),
        ],
        out_specs=(pl.BlockSpec((1, m, wk), lambda bi: (bi, 0, 0)),
                   pl.BlockSpec((1, m, wk), lambda bi: (bi, 0, 0))),
        compiler_params=pltpu.CompilerParams(
            dimension_semantics=("arbitrary",),
            vmem_limit_bytes=VMEM_LIMIT_BYTES),
        name="mem_kv",
    )(mem, g, w, kg)


def _merge_kernel(of_ref, ob_ref, ggate_ref, nb_ref, ngate_ref, mq_ref, mgate_ref, sig_ref,
                  x_ref, kc_ref, vc_ref, gout_ref, pa_ref, pb_ref, pc_ref, wo_ref, out_ref):
    d = x_ref.shape[1]
    o = of_ref[...].astype(F32) + ob_ref[...].astype(F32)
    parts = []
    for h in range(GLA_HEADS):
        oh = o[:, h * GLA_DV:(h + 1) * GLA_DV]
        ms = jnp.mean(oh * oh, axis=-1, keepdims=True)
        parts.append(oh * lax.rsqrt(ms + EPS) * gout_ref[...])
    oa = jnp.concatenate(parts, axis=1) * ggate_ref[...].astype(F32)
    ya = _dot(oa.astype(BF16), pa_ref[...])

    nb = nb_ref[...].astype(F32) * ngate_ref[...].astype(F32)
    yb = _dot(nb.astype(BF16), pb_ref[...])

    parts = []
    for h in range(MEM_HEADS):
        hs = slice(h * MEM_HD, (h + 1) * MEM_HD)
        s = _dot_nt(mq_ref[:, hs], kc_ref[0, :, hs])
        m = jnp.max(s, axis=-1, keepdims=True)
        pe = jnp.exp(s - m)
        den = jnp.sum(pe, axis=-1, keepdims=True)
        parts.append(_dot(pe.astype(BF16), vc_ref[0, :, hs]) / den)
    oc = jnp.concatenate(parts, axis=1) * mgate_ref[...].astype(F32)
    yc = _dot(oc.astype(BF16), pc_ref[...])

    y = (sig_ref[:, 0:d].astype(F32) * ya + sig_ref[:, d:2 * d].astype(F32) * yb
         + sig_ref[:, 2 * d:3 * d].astype(F32) * yc)
    out_ref[...] = x_ref[...] + _dot(y.astype(BF16), wo_ref[...])


def _merge(of2, ob2, nb2, proj2, x2, kc, vc, gout, pa, pb, pc, wo, layer, tm, seq):
    m, d = x2.shape
    v_w = GLA_HEADS * GLA_DV
    na_w = NA_HEADS * NA_HD
    mem_w = MEM_HEADS * MEM_HD
    steps_per_batch = seq // tm
    row = lambda col: (lambda i: (i, col))
    const = lambda i: (0, 0)
    weight = lambda w: pl.BlockSpec((None,) + w.shape[1:], lambda i: (layer, 0, 0))
    batch = lambda i: (i // steps_per_batch, 0, 0)
    return pl.pallas_call(
        _merge_kernel,
        out_shape=jax.ShapeDtypeStruct((m, d), F32),
        grid=(m // tm,),
        in_specs=[
            pl.BlockSpec((tm, v_w), row(0)),
            pl.BlockSpec((tm, v_w), row(0)),
            pl.BlockSpec((tm, v_w), row(T_GGATE * COL_TILE // v_w)),
            pl.BlockSpec((tm, na_w), row(0)),
            pl.BlockSpec((tm, na_w), row(T_NGATE)),
            pl.BlockSpec((tm, mem_w), row(T_MQ)),
            pl.BlockSpec((tm, mem_w), row(T_MGATE)),
            pl.BlockSpec((tm, 3 * d), row(T_MERGE * COL_TILE // (3 * d))),
            pl.BlockSpec((tm, d), row(0)),
            pl.BlockSpec((1,) + kc.shape[1:], batch),
            pl.BlockSpec((1,) + vc.shape[1:], batch),
            pl.BlockSpec((1, GLA_DV), const),
            weight(pa), weight(pb), weight(pc), weight(wo),
        ],
        out_specs=pl.BlockSpec((tm, d), row(0)),
        compiler_params=pltpu.CompilerParams(
            dimension_semantics=("arbitrary",),
            vmem_limit_bytes=VMEM_LIMIT_BYTES),
        name="merge",
    )(of2, ob2, proj2, nb2, proj2, proj2, proj2, proj2, x2, kc, vc, gout, pa, pb, pc, wo)


def _stacked_weights(w_in, na_rpb, w_mem_kv, p_a, p_b, p_c, w_out):
    qk_w = GLA_HEADS * GLA_DK
    lr0 = 2 * qk_w + 2 * GLA_HEADS * GLA_DV
    lr1 = lr0 + 2 * GLA_RANK
    w_main = jnp.concatenate([w_in[:, :, :lr0], w_in[:, :, lr1:]], axis=2).astype(BF16)
    w_lr = jnp.pad(w_in[:, :, lr0:lr1], ((0, 0), (0, 0), (0, 128 - 2 * GLA_RANK))).astype(BF16)
    return dict(w_main=w_main, w_lr=w_lr, t2=_na_bias_table(na_rpb),
                w_kv=w_mem_kv.astype(BF16), pa=p_a.astype(BF16), pb=p_b.astype(BF16),
                pc=p_c.astype(BF16), wo=w_out.astype(BF16))


def _layer_params(l, norm_g, gla_w2_f, gla_b_f, gla_w2_b, gla_b_b, gla_out_g, na_q_g, na_k_g,
                  mem_norm_g, mem_q_g, mem_k_g):
    d = norm_g.shape[1]
    qk_w = GLA_HEADS * GLA_DK
    gains = jnp.ones((N_COL_TILES, COL_TILE), F32)
    gains = gains.at[T_GQ].set(GLA_DK ** -0.5)
    gains = gains.at[T_NQ].set(jnp.tile(na_q_g[l].astype(F32) * NA_HD ** -0.5, NA_HEADS))
    gains = gains.at[T_NK].set(jnp.tile(na_k_g[l].astype(F32), NA_HEADS))
    gains = gains.at[T_MQ].set(jnp.tile(mem_q_g[l].astype(F32) * MEM_HD ** -0.5, MEM_HEADS))

    w2f = jnp.zeros((128, qk_w), F32).at[:GLA_RANK].set(gla_w2_f[l]).astype(BF16)
    w2b = jnp.zeros((128, qk_w), F32).at[GLA_RANK:2 * GLA_RANK].set(gla_w2_b[l]).astype(BF16)
    return dict(
        norm_g=norm_g[l].reshape(1, d).astype(F32),
        gains=gains.reshape(N_MAIN // IN_PROJ_WIDE_COLS, 1, IN_PROJ_WIDE_COLS),
        w2f=w2f, bf=gla_b_f[l].reshape(1, qk_w).astype(F32),
        w2b=w2b, bb=gla_b_b[l].reshape(1, qk_w).astype(F32),
        gout=gla_out_g[l].reshape(1, GLA_DV).astype(F32),
        mem_g=mem_norm_g[l].reshape(1, d).astype(F32),
        mem_kg=mem_k_g[l].reshape(1, MEM_HD).astype(F32),
    )


def _tiles(batch, seq):
    m = batch * seq
    tm_in = 2048 if m % 2048 == 0 else seq
    tm_merge = 512 if seq % 512 == 0 else seq
    tb_gla = 512 if seq % 512 == 0 else seq
    rb_na = 8
    return tm_in, tm_merge, tb_gla, rb_na


def kernel(x, mem, norm_g, w_in, gla_w2_f, gla_b_f, gla_w2_b, gla_b_b, gla_out_g, p_a, na_q_g,
           na_k_g, na_rpb, p_b, mem_norm_g, w_mem_kv, mem_q_g, mem_k_g, p_c, w_out):
    b, s, d = x.shape
    depth = w_in.shape[0]
    n_rows = s // GRID_W
    assert s % GRID_W == 0 and n_rows >= NA_ROWS and s % GLA_CHUNK == 0
    assert w_in.shape[2] - 2 * GLA_RANK == N_MAIN
    tm_in, tm_merge, tb_gla, rb_na = _tiles(b, s)
    tri, msk = _gla_constants()

    big = _stacked_weights(w_in, na_rpb, w_mem_kv, p_a, p_b, p_c, w_out)
    x2 = x.reshape(b * s, d).astype(F32)
    for l in range(depth):
        p = _layer_params(l, norm_g, gla_w2_f, gla_b_f, gla_w2_b, gla_b_b, gla_out_g, na_q_g,
                          na_k_g, mem_norm_g, mem_q_g, mem_k_g)
        proj2, glr2 = _in_proj(x2, p["norm_g"], big["w_main"], big["w_lr"], p["gains"], l, tm_in)
        proj3 = proj2.reshape(b, s, N_MAIN)
        glr3 = glr2.reshape(b, s, 128)
        o_f, o_b = _gla(proj3, glr3, p["w2f"], p["bf"], p["w2b"], p["bb"], tri, msk, tb_gla)
        nb = _na(proj3, big["t2"], l, rb_na)
        kc, vc = _mem_kv(mem.astype(F32), p["mem_g"], big["w_kv"], p["mem_kg"], l)
        x2 = _merge(o_f.reshape(b * s, -1), o_b.reshape(b * s, -1), nb.reshape(b * s, -1),
                    proj2, x2, kc, vc, p["gout"], big["pa"], big["pb"], big["pc"], big["wo"],
                    l, tm_merge, s)
    return x2.reshape(b, s, d).astype(x.dtype)
```
